```python
import math
import jax, jax.numpy as jnp
from jax import lax
import numpy as np

D_MODEL = 2048
BATCH = 8
SEQ = 4096
DEPTH = 1
DEC_BATCH = 8
DEC_SEQ = 32
PAST_LEN = 2048

CHUNK = 64
Q_BLOCK = 128
EPS = 1e-6
N_HEADS = D_MODEL // 128
Q_LORA = 768
KV_LORA = 512
QK_NOPE = 128
QK_ROPE = 64
V_DIM = 128
ROPE_THETA = 10000.0
ATTN_SCALE = 1.0 / math.sqrt(QK_NOPE + QK_ROPE)
S5_WIDTH = D_MODEL
S5_GROUP = 16
S5_GROUPS = S5_WIDTH // S5_GROUP
S5_STATE = 64
D_FF = 4 * D_MODEL
IN_SPLITS = [Q_LORA, Q_LORA + KV_LORA, Q_LORA + KV_LORA + QK_ROPE,
             Q_LORA + KV_LORA + QK_ROPE + S5_WIDTH,
             Q_LORA + KV_LORA + QK_ROPE + S5_WIDTH + D_MODEL]
IN_COLS = Q_LORA + KV_LORA + QK_ROPE + S5_WIDTH + 2 * D_MODEL

kernel_name = 'hybrid_mla_s5_streaming_step'


def rmsnorm(x, g):
    xf = x.astype(jnp.float32)
    y = xf * lax.rsqrt(jnp.mean(xf * xf, axis=-1, keepdims=True) + EPS)
    return (y * g.astype(jnp.float32)).astype(x.dtype)


def rope(x, pos):
    half = x.shape[-1] // 2
    inv = ROPE_THETA ** (-jnp.arange(half, dtype=jnp.float32) / half)
    ang = pos.astype(jnp.float32)[:, None] * inv[None, :]
    cos = jnp.cos(ang)[None, :, None, :]
    sin = jnp.sin(ang)[None, :, None, :]
    x1 = x[..., :half].astype(jnp.float32)
    x2 = x[..., half:].astype(jnp.float32)
    return jnp.concatenate([x1 * cos - x2 * sin, x1 * sin + x2 * cos], axis=-1).astype(x.dtype)


def mixer_inputs(x, pos, norm_mix, w_in, norm_q, w_uq, norm_kv):
    bsz, L, _ = x.shape
    z = rmsnorm(x, norm_mix) @ w_in
    c_q, c_kv, k_r, u, g_a, g_b = jnp.split(z, IN_SPLITS, axis=-1)
    q = (rmsnorm(c_q, norm_q) @ w_uq).reshape(bsz, L, N_HEADS, QK_NOPE + QK_ROPE)
    q_nope = q[..., :QK_NOPE]
    q_rope = rope(q[..., QK_NOPE:], pos)
    ckv = rmsnorm(c_kv, norm_kv)
    krope = rope(k_r[:, :, None, :], pos)[:, :, 0, :]
    return q_nope, q_rope, ckv, krope, u, jax.nn.sigmoid(g_a), jax.nn.sigmoid(g_b)


def latent_attention_block(q_nope, q_rope, k_nope, krope, v, q_pos, k_pos):
    s = (jnp.einsum('bqhd,bkhd->bhqk', q_nope, k_nope).astype(jnp.float32)
         + jnp.einsum('bqhr,bkr->bhqk', q_rope, krope).astype(jnp.float32)) * ATTN_SCALE
    mask = (k_pos // CHUNK)[None, :] <= (q_pos // CHUNK)[:, None]
    p = jax.nn.softmax(jnp.where(mask[None, None], s, -jnp.inf), axis=-1).astype(v.dtype)
    return jnp.einsum('bhqk,bkhd->bqhd', p, v)


def mla_attention(q_nope, q_rope, ckv, krope, q_pos, k_pos, w_uk, w_uv):
    bsz, L = q_nope.shape[:2]
    k_nope = jnp.einsum('bkc,chd->bkhd', ckv, w_uk)
    v = jnp.einsum('bkc,chd->bkhd', ckv, w_uv)
    if L > Q_BLOCK and L % Q_BLOCK == 0:
        nb = L // Q_BLOCK
        qn = q_nope.reshape(bsz, nb, Q_BLOCK, N_HEADS, QK_NOPE).swapaxes(0, 1)
        qr = q_rope.reshape(bsz, nb, Q_BLOCK, N_HEADS, QK_ROPE).swapaxes(0, 1)
        qp = q_pos.reshape(nb, Q_BLOCK)
        o = lax.map(lambda a: latent_attention_block(a[0], a[1], k_nope, krope, v, a[2], k_pos), (qn, qr, qp))
        o = o.swapaxes(0, 1).reshape(bsz, L, N_HEADS, V_DIM)
    else:
        o = latent_attention_block(q_nope, q_rope, k_nope, krope, v, q_pos, k_pos)
    return o.reshape(bsz, L, N_HEADS * V_DIM)


def s5_discretize(a_re, a_im, log_dt, b_re, b_im):
    ar = a_re.astype(jnp.float32)
    ai = a_im.astype(jnp.float32)
    dt = jnp.exp(log_dt.astype(jnp.float32))
    mag = jnp.exp(ar * dt)
    abar_re = mag * jnp.cos(ai * dt)
    abar_im = mag * jnp.sin(ai * dt)
    den = ar * ar + ai * ai
    nr = abar_re - 1.0
    coef_re = (nr * ar + abar_im * ai) / den
    coef_im = (abar_im * ar - nr * ai) / den
    br = b_re.astype(jnp.float32)
    bi = b_im.astype(jnp.float32)
    bbar_re = coef_re[..., None] * br - coef_im[..., None] * bi
    bbar_im = coef_re[..., None] * bi + coef_im[..., None] * br
    return abar_re, abar_im, bbar_re, bbar_im


def _complex_affine_combine(e1, e2):
    a1r, a1i, b1r, b1i = e1
    a2r, a2i, b2r, b2i = e2
    return (a2r * a1r - a2i * a1i, a2r * a1i + a2i * a1r,
            a2r * b1r - a2i * b1i + b2r, a2r * b1i + a2i * b1r + b2i)


def s5_scan(u, h0_re, h0_im, abar_re, abar_im, bbar_re, bbar_im, c_re, c_im):
    bsz, L, _ = u.shape
    blk = CHUNK if L % CHUNK == 0 else L
    nb = L // blk
    ub = u.astype(jnp.float32).reshape(bsz, nb, blk, S5_GROUPS, S5_GROUP).swapaxes(0, 1)
    cr = c_re.astype(jnp.float32)
    ci = c_im.astype(jnp.float32)

    def step(carry, u_blk):
        hr, hi = carry
        bur = jnp.einsum('blgm,gpm->blgp', u_blk, bbar_re)
        bui = jnp.einsum('blgm,gpm->blgp', u_blk, bbar_im)
        bur = bur.at[:, 0].add(abar_re * hr - abar_im * hi)
        bui = bui.at[:, 0].add(abar_re * hi + abar_im * hr)
        ar = jnp.broadcast_to(abar_re, bur.shape)
        ai = jnp.broadcast_to(abar_im, bur.shape)
        _, _, sr, si = lax.associative_scan(_complex_affine_combine, (ar, ai, bur, bui), axis=1)
        y = jnp.einsum('blgp,gmp->blgm', sr, cr) - jnp.einsum('blgp,gmp->blgm', si, ci)
        return (sr[:, -1], si[:, -1]), y

    (hr, hi), ys = lax.scan(step, (h0_re.astype(jnp.float32), h0_im.astype(jnp.float32)), ub)
    return ys.swapaxes(0, 1).reshape(bsz, L, S5_WIDTH), hr, hi


def s5_branch(u, h0_re, h0_im, disc, c_re, c_im, d_skip, w_glu_a, w_glu_b):
    abar_re, abar_im, bbar_re, bbar_im = disc
    y, hr, hi = s5_scan(u, h0_re, h0_im, abar_re, abar_im, bbar_re, bbar_im, c_re, c_im)
    y = y.astype(u.dtype) + d_skip * u
    g = jax.nn.gelu(y)
    return (g @ w_glu_a) * jax.nn.sigmoid(g @ w_glu_b), hr, hi


def merge_and_mlp(x, attn, s5_out, g_a, g_b, w_o, norm_mlp, w_up, w_down):
    h = x + (g_a * attn + g_b * s5_out) @ w_o
    return h + jnp.square(jax.nn.relu(rmsnorm(h, norm_mlp) @ w_up)) @ w_down


def setup_inputs(seed: int = 0) -> dict:
    key = jax.random.key(seed)
    ks = jax.random.split(key, 32)
    f32 = jnp.float32

    def nrm(k, shape, scale):
        return jax.random.normal(k, shape, f32) * scale

    def gain(k, n):
        return 1.0 + 0.01 * jax.random.normal(k, (DEPTH, n), f32)

    n_idx = jnp.arange(S5_STATE, dtype=f32)
    a_re = -0.5 + 0.01 * jax.random.normal(ks[6], (DEPTH, S5_GROUPS, S5_STATE), f32)
    a_im = math.pi * n_idx + 0.01 * jax.random.normal(ks[7], (DEPTH, S5_GROUPS, S5_STATE), f32)
    log_dt = jax.random.uniform(ks[8], (DEPTH, S5_GROUPS, S5_STATE), f32,
                                minval=math.log(0.001), maxval=math.log(0.1))
    return {
        'x_prompt': nrm(ks[0], (BATCH, SEQ, D_MODEL), 1.0),
        'x_sample': nrm(ks[1], (DEC_BATCH, DEC_SEQ, D_MODEL), 1.0),
        'cache_ckv': nrm(ks[2], (DEPTH, DEC_BATCH, PAST_LEN, KV_LORA), 1.0),
        'cache_krope': nrm(ks[3], (DEPTH, DEC_BATCH, PAST_LEN, QK_ROPE), 1.0),
        'state_s5_re': nrm(ks[4], (DEPTH, DEC_BATCH, S5_GROUPS, S5_STATE), 0.1),
        'state_s5_im': nrm(ks[5], (DEPTH, DEC_BATCH, S5_GROUPS, S5_STATE), 0.1),
        'norm_mix': gain(ks[9], D_MODEL),
        'w_in': nrm(ks[10], (DEPTH, D_MODEL, IN_COLS), D_MODEL ** -0.5),
        'norm_q': gain(ks[11], Q_LORA),
        'w_uq': nrm(ks[12], (DEPTH, Q_LORA, N_HEADS * (QK_NOPE + QK_ROPE)), Q_LORA ** -0.5),
        'norm_kv': gain(ks[13], KV_LORA),
        'w_uk': nrm(ks[14], (DEPTH, KV_LORA, N_HEADS, QK_NOPE), KV_LORA ** -0.5),
        'w_uv': nrm(ks[15], (DEPTH, KV_LORA, N_HEADS, V_DIM), KV_LORA ** -0.5),
        's5_a_re': a_re,
        's5_a_im': a_im,
        's5_log_dt': log_dt,
        's5_b_re': nrm(ks[16], (DEPTH, S5_GROUPS, S5_STATE, S5_GROUP), (2.0 * S5_GROUP) ** -0.5),
        's5_b_im': nrm(ks[17], (DEPTH, S5_GROUPS, S5_STATE, S5_GROUP), (2.0 * S5_GROUP) ** -0.5),
        's5_c_re': nrm(ks[18], (DEPTH, S5_GROUPS, S5_GROUP, S5_STATE), (2.0 * S5_STATE) ** -0.5),
        's5_c_im': nrm(ks[19], (DEPTH, S5_GROUPS, S5_GROUP, S5_STATE), (2.0 * S5_STATE) ** -0.5),
        's5_d': nrm(ks[20], (DEPTH, S5_WIDTH), 1.0),
        'w_glu_a': nrm(ks[21], (DEPTH, S5_WIDTH, D_MODEL), S5_WIDTH ** -0.5),
        'w_glu_b': nrm(ks[22], (DEPTH, S5_WIDTH, D_MODEL), S5_WIDTH ** -0.5),
        'w_o': nrm(ks[23], (DEPTH, D_MODEL, D_MODEL), D_MODEL ** -0.5),
        'norm_mlp': gain(ks[24], D_MODEL),
        'w_up': nrm(ks[25], (DEPTH, D_MODEL, D_FF), D_MODEL ** -0.5),
        'w_down': nrm(ks[26], (DEPTH, D_FF, D_MODEL), D_FF ** -0.5),
        'norm_final': 1.0 + 0.01 * jax.random.normal(ks[27], (D_MODEL,), f32),
    }


def reference(x_prompt, x_sample, cache_ckv, cache_krope, state_s5_re, state_s5_im,
              norm_mix, w_in, norm_q, w_uq, norm_kv, w_uk, w_uv,
              s5_a_re, s5_a_im, s5_log_dt, s5_b_re, s5_b_im, s5_c_re, s5_c_im, s5_d,
              w_glu_a, w_glu_b, w_o, norm_mlp, w_up, w_down, norm_final):
    bsz_p, L_p, _ = x_prompt.shape
    L_s = x_sample.shape[1]
    past_len = cache_ckv.shape[2]
    pos_p = jnp.arange(L_p)
    pos_s = past_len + jnp.arange(L_s)
    k_pos_s = jnp.arange(past_len + L_s)
    sdt = state_s5_re.dtype

    hp, hs = x_prompt, x_sample
    ckv_p_l, kr_p_l, sre_p_l, sim_p_l = [], [], [], []
    ckv_s_l, kr_s_l, sre_s_l, sim_s_l = [], [], [], []
    for l in range(DEPTH):
        disc = s5_discretize(s5_a_re[l], s5_a_im[l], s5_log_dt[l], s5_b_re[l], s5_b_im[l])

        qn, qr, ckv, kr, u, ga, gb = mixer_inputs(hp, pos_p, norm_mix[l], w_in[l], norm_q[l], w_uq[l], norm_kv[l])
        attn = mla_attention(qn, qr, ckv, kr, pos_p, pos_p, w_uk[l], w_uv[l])
        h0 = jnp.zeros((bsz_p, S5_GROUPS, S5_STATE), jnp.float32)
        s5o, sr, si = s5_branch(u, h0, h0, disc, s5_c_re[l], s5_c_im[l], s5_d[l], w_glu_a[l], w_glu_b[l])
        hp = merge_and_mlp(hp, attn, s5o, ga, gb, w_o[l], norm_mlp[l], w_up[l], w_down[l])
        ckv_p_l.append(ckv)
        kr_p_l.append(kr)
        sre_p_l.append(sr.astype(sdt))
        sim_p_l.append(si.astype(sdt))

        qn, qr, ckv, kr, u, ga, gb = mixer_inputs(hs, pos_s, norm_mix[l], w_in[l], norm_q[l], w_uq[l], norm_kv[l])
        ckv_all = jnp.concatenate([cache_ckv[l].astype(ckv.dtype), ckv], axis=1)
        kr_all = jnp.concatenate([cache_krope[l].astype(kr.dtype), kr], axis=1)
        attn = mla_attention(qn, qr, ckv_all, kr_all, pos_s, k_pos_s, w_uk[l], w_uv[l])
        s5o, sr, si = s5_branch(u, state_s5_re[l], state_s5_im[l], disc, s5_c_re[l], s5_c_im[l], s5_d[l],
                                w_glu_a[l], w_glu_b[l])
        hs = merge_and_mlp(hs, attn, s5o, ga, gb, w_o[l], norm_mlp[l], w_up[l], w_down[l])
        ckv_s_l.append(ckv)
        kr_s_l.append(kr)
        sre_s_l.append(sr.astype(sdt))
        sim_s_l.append(si.astype(sdt))

    y_prompt = rmsnorm(hp, norm_final)
    y_sample = rmsnorm(hs, norm_final)
    ckv_prompt = jnp.stack(ckv_p_l, axis=0)
    krope_prompt = jnp.stack(kr_p_l, axis=0)
    s5_re_prompt = jnp.stack(sre_p_l, axis=0)
    s5_im_prompt = jnp.stack(sim_p_l, axis=0)
    ckv_sample = jnp.stack(ckv_s_l, axis=0)
    krope_sample = jnp.stack(kr_s_l, axis=0)
    s5_re_sample = jnp.stack(sre_s_l, axis=0)
    s5_im_sample = jnp.stack(sim_s_l, axis=0)
    return (y_prompt, y_sample, ckv_prompt, krope_prompt, s5_re_prompt, s5_im_prompt,
            ckv_sample, krope_sample, s5_re_sample, s5_im_sample)
```

```python
import functools
import math

import jax
import jax.numpy as jnp
from jax import lax
from jax.experimental import pallas as pl
from jax.experimental.pallas import tpu as pltpu

F32 = jnp.float32
BF16 = jnp.bfloat16

EPS = 1e-6
CHUNK = 64
ROPE_THETA = 10000.0
LANES = 128
SUBLANES = 8
VMEM_LIMIT = 56 * 1024 * 1024


def _cparams(n_axes):
    return pltpu.CompilerParams(dimension_semantics=("arbitrary",) * n_axes,
                                vmem_limit_bytes=VMEM_LIMIT)


def _const_spec(shape):
    nd = len(shape)
    return pl.BlockSpec(shape, lambda *_: (0,) * nd, pipeline_mode=pl.Buffered(1))


def _rms(x):
    return x * lax.rsqrt(jnp.mean(x * x, axis=-1, keepdims=True) + EPS)


def _pick_tile(n, cap, mult):
    best = None
    for t in range(mult, min(n, cap) + 1, mult):
        if n % t == 0:
            best = t
    assert best is not None, (n, cap, mult)
    return best


def _disc_kernel(ar_ref, ai_ref, ldt_ref, br_ref, bi_ref, abr_ref, abi_ref, bbr_ref, bbi_ref):
    ar = ar_ref[...]
    ai = ai_ref[...]
    dt = jnp.exp(ldt_ref[...])
    mag = jnp.exp(ar * dt)
    abar_re = mag * jnp.cos(ai * dt)
    abar_im = mag * jnp.sin(ai * dt)
    den = ar * ar + ai * ai
    nr = abar_re - 1.0
    coef_re = (nr * ar + abar_im * ai) / den
    coef_im = (abar_im * ar - nr * ai) / den
    br = br_ref[...]
    bi = bi_ref[...]
    abr_ref[...] = abar_re
    abi_ref[...] = abar_im
    bbr_ref[...] = coef_re * br - coef_im * bi
    bbi_ref[...] = coef_re * bi + coef_im * br


def _s5_discretize(a_re, a_im, log_dt, b_re, b_im):
    g, p, m = b_re.shape
    rep = lambda a: jnp.repeat(a.astype(F32), m, axis=1)
    flat = lambda b: b.astype(F32).reshape(g, p * m)
    shp = jax.ShapeDtypeStruct((g, p * m), F32)
    abr, abi, bbr, bbi = pl.pallas_call(
        _disc_kernel, out_shape=(shp, shp, shp, shp), name="s5_disc",
    )(rep(a_re), rep(a_im), rep(log_dt), flat(b_re), flat(b_im))
    return abr[:, ::m], abi[:, ::m], bbr.reshape(g, p, m), bbi.reshape(g, p, m)


def _proj_kernel(x_ref, tab_ref, gmix_ref, wlat_ref, gq_ref, wuq_ref, gkv_ref, wu_ref,
                 q_ref, ckv_ref, kr_ref, u_ref, *, q_lora, kv_lora, n_heads, scale):
    x = x_ref[...]
    xn = (_rms(x) * gmix_ref[...]).astype(BF16)
    z = jnp.dot(xn, wlat_ref[...], preferred_element_type=F32)
    cqn = (_rms(z[:, :q_lora]) * gq_ref[...]).astype(BF16)
    tab = tab_ref[...]
    for h in range(n_heads):
        qh = jnp.dot(cqn, wuq_ref[:, h * 256:(h + 1) * 256], preferred_element_type=F32) * scale
        q_ref[h, :, 0:LANES] = qh[:, :LANES].astype(BF16)
        r = qh[:, LANES:] * tab
        q_ref[h, :, LANES:2 * LANES] = (r + pltpu.roll(r, 64, 1)).astype(BF16)
    ckv_ref[...] = _rms(z[:, q_lora:q_lora + kv_lora]) * gkv_ref[...]
    kr = z[:, q_lora + kv_lora:] * tab
    kr = kr + pltpu.roll(kr, 64, 1)
    kr_ref[...] = kr[:, :64]
    u_ref[...] = jnp.dot(xn, wu_ref[...], preferred_element_type=F32)


def _proj(x2, tab, tab_blocks, gmix, wlat, gq, wuq, gkv, wu, *, tile, n_heads, q_lora, kv_lora, rope, scale):
    t_rows, d = x2.shape
    n_tiles = t_rows // tile
    kern = functools.partial(_proj_kernel, q_lora=q_lora, kv_lora=kv_lora, n_heads=n_heads, scale=scale)
    return pl.pallas_call(
        kern,
        grid=(n_tiles,),
        in_specs=[
            pl.BlockSpec((tile, d), lambda r: (r, 0)),
            pl.BlockSpec((tile, LANES), lambda r: (r % tab_blocks, 0)),
            _const_spec(gmix.shape), _const_spec(wlat.shape), _const_spec(gq.shape),
            _const_spec(wuq.shape), _const_spec(gkv.shape), _const_spec(wu.shape),
        ],
        out_specs=[
            pl.BlockSpec((n_heads, tile, 2 * LANES), lambda r: (0, r, 0)),
            pl.BlockSpec((tile, kv_lora), lambda r: (r, 0)),
            pl.BlockSpec((tile, rope), lambda r: (r, 0)),
            pl.BlockSpec((tile, wu.shape[1]), lambda r: (r, 0)),
        ],
        out_shape=[
            jax.ShapeDtypeStruct((n_heads, t_rows, 2 * LANES), BF16),
            jax.ShapeDtypeStruct((t_rows, kv_lora), F32),
            jax.ShapeDtypeStruct((t_rows, rope), F32),
            jax.ShapeDtypeStruct((t_rows, wu.shape[1]), F32),
        ],
        compiler_params=_cparams(1),
        name="proj",
    )(x2, tab, gmix, wlat, gq, wuq, gkv, wu)


def _kvup_kernel(ckv_ref, kr_ref, wuk_ref, wuv_ref, k_ref, v_ref, *, n_heads):
    c = ckv_ref[...].astype(BF16)
    kr = kr_ref[...]
    krz = jnp.concatenate([kr, jnp.zeros_like(kr)], axis=1).astype(BF16)
    for hp in range(n_heads // 2):
        kn = jnp.dot(c, wuk_ref[:, hp * 256:(hp + 1) * 256], preferred_element_type=F32).astype(BF16)
        vv = jnp.dot(c, wuv_ref[:, hp * 256:(hp + 1) * 256], preferred_element_type=F32).astype(BF16)
        for s in range(2):
            h = 2 * hp + s
            k_ref[h, :, 0:LANES] = kn[:, s * LANES:(s + 1) * LANES]
            k_ref[h, :, LANES:2 * LANES] = krz
            v_ref[h] = vv[:, s * LANES:(s + 1) * LANES]


def _kvup(ckv2, kr2, wuk, wuv, *, tile, n_heads):
    t_rows, kv_lora = ckv2.shape
    rope = kr2.shape[1]
    kern = functools.partial(_kvup_kernel, n_heads=n_heads)
    return pl.pallas_call(
        kern,
        grid=(t_rows // tile,),
        in_specs=[
            pl.BlockSpec((tile, kv_lora), lambda r: (r, 0)),
            pl.BlockSpec((tile, rope), lambda r: (r, 0)),
            _const_spec(wuk.shape), _const_spec(wuv.shape),
        ],
        out_specs=[
            pl.BlockSpec((n_heads, tile, 2 * LANES), lambda r: (0, r, 0)),
            pl.BlockSpec((n_heads, tile, LANES), lambda r: (0, r, 0)),
        ],
        out_shape=[
            jax.ShapeDtypeStruct((n_heads, t_rows, 2 * LANES), BF16),
            jax.ShapeDtypeStruct((n_heads, t_rows, LANES), BF16),
        ],
        compiler_params=_cparams(1),
        name="kvup",
    )(ckv2, kr2, wuk, wuv)


def _attn_tile(q, k, v, m_ref, l_ref, acc_ref, mask):
    s = lax.dot_general(q, k, (((1,), (1,)), ((), ())), preferred_element_type=F32)
    if mask is not None:
        s = jnp.where(mask, s, -jnp.inf)
    m_prev = m_ref[...]
    m_new = jnp.maximum(m_prev, jnp.max(s, axis=-1, keepdims=True))
    alpha = jnp.exp(m_prev - m_new)
    p = jnp.exp(s - m_new)
    l_ref[...] = alpha * l_ref[...] + jnp.sum(p, axis=-1, keepdims=True)
    acc_ref[...] = alpha * acc_ref[...] + jnp.dot(p.astype(BF16), v, preferred_element_type=F32)
    m_ref[...] = m_new


def _chunk_mask(tq, tk, q_off, k_off):
    qc = (lax.broadcasted_iota(jnp.int32, (tq, tk), 0) + q_off) // CHUNK
    kc = (lax.broadcasted_iota(jnp.int32, (tq, tk), 1) + k_off) // CHUNK
    return kc <= qc


def _attn_causal_kernel(q_ref, k_ref, v_ref, o_ref, m_ref, l_ref, acc_ref, *, tile):
    qi = pl.program_id(2)
    q = q_ref[...]
    m_ref[...] = jnp.full(m_ref.shape, -jnp.inf, F32)
    l_ref[...] = jnp.zeros(l_ref.shape, F32)
    acc_ref[...] = jnp.zeros(acc_ref.shape, F32)

    def full_tile(j, carry):
        off = pl.multiple_of(j * tile, tile)
        _attn_tile(q, k_ref[pl.ds(off, tile), :], v_ref[pl.ds(off, tile), :], m_ref, l_ref, acc_ref, None)
        return carry

    lax.fori_loop(0, qi, full_tile, 0)
    off = pl.multiple_of(qi * tile, tile)
    _attn_tile(q, k_ref[pl.ds(off, tile), :], v_ref[pl.ds(off, tile), :], m_ref, l_ref, acc_ref,
               _chunk_mask(tile, tile, 0, 0))
    o_ref[...] = (acc_ref[...] / l_ref[...]).astype(o_ref.dtype)


def _attn_single_kernel(q_ref, k_ref, v_ref, o_ref, m_ref, l_ref, acc_ref, *, q_pos0):
    tq, tk = q_ref.shape[0], k_ref.shape[0]
    m_ref[...] = jnp.full(m_ref.shape, -jnp.inf, F32)
    l_ref[...] = jnp.zeros(l_ref.shape, F32)
    acc_ref[...] = jnp.zeros(acc_ref.shape, F32)
    _attn_tile(q_ref[...], k_ref[...], v_ref[...], m_ref, l_ref, acc_ref, _chunk_mask(tq, tk, q_pos0, 0))
    o_ref[...] = (acc_ref[...] / l_ref[...]).astype(o_ref.dtype)


def _attention(q, k, v, *, n_batch, lq, lk, tile, q_pos0):
    n_heads = q.shape[0]
    causal = tile is not None
    tq = tile if causal else lq
    nq = lq // tq
    if causal:
        assert lq == lk and q_pos0 == 0 and tile % CHUNK == 0
        kern = functools.partial(_attn_causal_kernel, tile=tile)
    else:
        kern = functools.partial(_attn_single_kernel, q_pos0=q_pos0)
    return pl.pallas_call(
        kern,
        grid=(n_batch, n_heads, nq),
        in_specs=[
            pl.BlockSpec((None, tq, 2 * LANES), lambda b, h, i: (h, b * nq + i, 0)),
            pl.BlockSpec((None, lk, 2 * LANES), lambda b, h, i: (h, b, 0)),
            pl.BlockSpec((None, lk, LANES), lambda b, h, i: (h, b, 0)),
        ],
        out_specs=pl.BlockSpec((None, tq, LANES), lambda b, h, i: (h, b * nq + i, 0)),
        out_shape=jax.ShapeDtypeStruct((n_heads, n_batch * lq, LANES), BF16),
        scratch_shapes=[pltpu.VMEM((tq, 1), F32), pltpu.VMEM((tq, 1), F32), pltpu.VMEM((tq, LANES), F32)],
        compiler_params=_cparams(3),
        name="attn_causal" if causal else "attn_single",
    )(q, k, v)


def _s5_kernel(u_ref, h0_ref, a_ref, wb_ref, wc_ref, d_ref, g_ref, hout_ref, x_ref, hs_ref, *, tl, n_batch):
    i = pl.program_id(1)
    half = a_ref.shape[-1] // 2
    rows = tl * n_batch

    @pl.when(i == 0)
    def _():
        hs_ref[...] = h0_ref[...]

    u = jnp.swapaxes(u_ref[...], 0, 1).reshape(rows, LANES)
    x_ref[...] = jnp.dot(u.astype(BF16), wb_ref[...], preferred_element_type=F32)
    ar = jnp.broadcast_to(a_ref[:, :half], (n_batch, half))
    ai = jnp.broadcast_to(a_ref[:, half:], (n_batch, half))

    def step(t, carry):
        hr, hi = carry
        r0 = pl.multiple_of(t * n_batch, n_batch)
        xr = x_ref[pl.ds(r0, n_batch), :half]
        xi = x_ref[pl.ds(r0, n_batch), half:]
        nr = ar * hr - ai * hi + xr
        ni = ar * hi + ai * hr + xi
        x_ref[pl.ds(r0, n_batch), :half] = nr
        x_ref[pl.ds(r0, n_batch), half:] = ni
        return nr, ni

    hr, hi = lax.fori_loop(0, tl, step, (hs_ref[:, :half], hs_ref[:, half:]), unroll=8)
    hs_ref[:, :half] = hr
    hs_ref[:, half:] = hi

    y = jnp.dot(x_ref[...].astype(BF16), wc_ref[...], preferred_element_type=F32)
    g = jax.nn.gelu(y + d_ref[...] * u)
    g_ref[...] = jnp.swapaxes(g.reshape(tl, n_batch, LANES), 0, 1).astype(g_ref.dtype)

    @pl.when(i == pl.num_programs(1) - 1)
    def _():
        hout_ref[...] = hs_ref[...]


def _s5(u3, h0, a_l, wb, wc, d_l, *, tl):
    n_batch, seq, width = u3.shape
    n_slabs = width // LANES
    st = a_l.shape[-1]
    kern = functools.partial(_s5_kernel, tl=tl, n_batch=n_batch)
    return pl.pallas_call(
        kern,
        grid=(n_slabs, seq // tl),
        in_specs=[
            pl.BlockSpec((n_batch, tl, LANES), lambda j, i: (0, i, j)),
            pl.BlockSpec((None, n_batch, st), lambda j, i: (j, 0, 0)),
            pl.BlockSpec((None, 1, st), lambda j, i: (j, 0, 0)),
            pl.BlockSpec((None, LANES, st), lambda j, i: (j, 0, 0)),
            pl.BlockSpec((None, st, LANES), lambda j, i: (j, 0, 0)),
            pl.BlockSpec((None, 1, LANES), lambda j, i: (j, 0, 0)),
        ],
        out_specs=[
            pl.BlockSpec((n_batch, tl, LANES), lambda j, i: (0, i, j)),
            pl.BlockSpec((None, n_batch, st), lambda j, i: (j, 0, 0)),
        ],
        out_shape=[
            jax.ShapeDtypeStruct((n_batch, seq, width), BF16),
            jax.ShapeDtypeStruct((n_slabs, n_batch, st), F32),
        ],
        scratch_shapes=[pltpu.VMEM((tl * n_batch, st), F32), pltpu.VMEM((n_batch, st), F32)],
        compiler_params=_cparams(2),
        name="s5",
    )(u3, h0, a_l, wb, wc, d_l)


def _merge_kernel(x_ref, g_ref, attn_ref, gmix_ref, wga_ref, wgb_ref, wa_ref, wb_ref, wo_ref,
                  h_ref, xn_ref, acc_ref):
    j = pl.program_id(1)

    @pl.when(j == 0)
    def _():
        xn_ref[...] = (_rms(x_ref[...]) * gmix_ref[...]).astype(BF16)
        acc_ref[...] = jnp.zeros(acc_ref.shape, F32)

    xn = xn_ref[...]
    g = g_ref[...]
    ga = jax.nn.sigmoid(jnp.dot(xn, wga_ref[...], preferred_element_type=F32))
    gb = jax.nn.sigmoid(jnp.dot(xn, wgb_ref[...], preferred_element_type=F32))
    s5o = (jnp.dot(g, wa_ref[...], preferred_element_type=F32)
           * jax.nn.sigmoid(jnp.dot(g, wb_ref[...], preferred_element_type=F32)))
    attn = jnp.concatenate([attn_ref[h].astype(F32) for h in range(attn_ref.shape[0])], axis=1)
    mix = (ga * attn + gb * s5o).astype(BF16)
    acc_ref[...] += jnp.dot(mix, wo_ref[...], preferred_element_type=F32)

    @pl.when(j == pl.num_programs(1) - 1)
    def _():
        h_ref[...] = x_ref[...] + acc_ref[...]


def _merge(x2, g2, attn, gmix, wga, wgb, wa, wb, wo, *, tile, cb):
    t_rows, d = x2.shape
    hpb = cb // LANES
    return pl.pallas_call(
        _merge_kernel,
        grid=(t_rows // tile, d // cb),
        in_specs=[
            pl.BlockSpec((tile, d), lambda r, j: (r, 0)),
            pl.BlockSpec((tile, d), lambda r, j: (r, 0)),
            pl.BlockSpec((hpb, tile, LANES), lambda r, j: (j, r, 0)),
            _const_spec(gmix.shape),
            pl.BlockSpec((d, cb), lambda r, j: (0, j)),
            pl.BlockSpec((d, cb), lambda r, j: (0, j)),
            pl.BlockSpec((d, cb), lambda r, j: (0, j)),
            pl.BlockSpec((d, cb), lambda r, j: (0, j)),
            pl.BlockSpec((cb, d), lambda r, j: (j, 0)),
        ],
        out_specs=pl.BlockSpec((tile, d), lambda r, j: (r, 0)),
        out_shape=jax.ShapeDtypeStruct((t_rows, d), F32),
        scratch_shapes=[pltpu.VMEM((tile, d), BF16), pltpu.VMEM((tile, d), F32)],
        compiler_params=_cparams(2),
        name="merge",
    )(x2, g2, attn, gmix, wga, wgb, wa, wb, wo)


def _mlp_kernel(h_ref, gmlp_ref, wup_ref, wdn_ref, gfin_ref, y_ref, hn_ref, acc_ref, *, final_norm):
    j = pl.program_id(1)

    @pl.when(j == 0)
    def _():
        hn_ref[...] = (_rms(h_ref[...]) * gmlp_ref[...]).astype(BF16)
        acc_ref[...] = jnp.zeros(acc_ref.shape, F32)

    a = jnp.maximum(jnp.dot(hn_ref[...], wup_ref[...], preferred_element_type=F32), 0.0)
    acc_ref[...] += jnp.dot((a * a).astype(BF16), wdn_ref[...], preferred_element_type=F32)

    @pl.when(j == pl.num_programs(1) - 1)
    def _():
        y = h_ref[...] + acc_ref[...]
        y_ref[...] = _rms(y) * gfin_ref[...] if final_norm else y


def _mlp(h2, gmlp, wup, wdn, gfin, *, tile, fb, final_norm):
    t_rows, d = h2.shape
    d_ff = wup.shape[1]
    return pl.pallas_call(
        functools.partial(_mlp_kernel, final_norm=final_norm),
        grid=(t_rows // tile, d_ff // fb),
        in_specs=[
            pl.BlockSpec((tile, d), lambda r, j: (r, 0)),
            _const_spec(gmlp.shape),
            pl.BlockSpec((d, fb), lambda r, j: (0, j)),
            pl.BlockSpec((fb, d), lambda r, j: (j, 0)),
            _const_spec(gfin.shape),
        ],
        out_specs=pl.BlockSpec((tile, d), lambda r, j: (r, 0)),
        out_shape=jax.ShapeDtypeStruct((t_rows, d), F32),
        scratch_shapes=[pltpu.VMEM((tile, d), BF16), pltpu.VMEM((tile, d), F32)],
        compiler_params=_cparams(2),
        name="mlp",
    )(h2, gmlp, wup, wdn, gfin)


def _rope_table(pos, rope):
    half = rope // 2
    inv = ROPE_THETA ** (-jnp.arange(half, dtype=F32) / half)
    ang = pos.astype(F32)[:, None] * inv[None, :]
    cos, sin = jnp.cos(ang), jnp.sin(ang)
    return jnp.concatenate([cos, cos, -sin, sin], axis=1)


def _swap_halves(w):
    half = w.shape[-1] // 2
    return jnp.concatenate([w[..., half:], w[..., :half]], axis=-1)


def _group(x, pos0, w, s5w, h0, cache, *, row_tile, attn_tile, s5_tl):
    n_batch, seq, d = x.shape
    t_rows = n_batch * seq
    n_heads, kv_lora, rope = w["n_heads"], w["kv_lora"], w["rope"]
    x2 = x.reshape(t_rows, d)

    pos = pos0 + jnp.arange(seq)
    tab = _rope_table(pos, rope)
    if row_tile > seq:
        assert row_tile % seq == 0
        tab = jnp.tile(tab, (row_tile // seq, 1))
        tab_blocks = 1
    else:
        assert seq % row_tile == 0
        tab_blocks = seq // row_tile

    q, ckv, kr, u = _proj(x2, tab, tab_blocks, w["gmix"], w["wlat"], w["gq"], w["wuq"], w["gkv"], w["wu"],
                          tile=row_tile, n_heads=n_heads, q_lora=w["q_lora"], kv_lora=kv_lora, rope=rope,
                          scale=w["scale"])
    ckv3 = ckv.reshape(n_batch, seq, kv_lora)
    kr3 = kr.reshape(n_batch, seq, rope)
    if cache is not None:
        ckv_all = jnp.concatenate([cache[0].astype(F32), ckv3], axis=1)
        kr_all = jnp.concatenate([cache[1].astype(F32), kr3], axis=1)
    else:
        ckv_all, kr_all = ckv3, kr3
    lk = ckv_all.shape[1]
    k_rows = n_batch * lk
    k, v = _kvup(ckv_all.reshape(k_rows, kv_lora), kr_all.reshape(k_rows, rope), w["wuk"], w["wuv"],
                 tile=_pick_tile(k_rows, 1024, 16), n_heads=n_heads)
    attn = _attention(q, k, v, n_batch=n_batch, lq=seq, lk=lk, tile=attn_tile, q_pos0=pos0)

    g, h_fin = _s5(u.reshape(n_batch, seq, -1), h0, s5w["a"], s5w["wb"], s5w["wc"], s5w["d"], tl=s5_tl)

    h = _merge(x2, g.reshape(t_rows, -1), attn, w["gmix"], w["wga"], w["wgb"], w["wglu_a"], w["wglu_b"], w["wo"],
               tile=row_tile, cb=4 * LANES)
    return h, ckv3, kr3, h_fin


def _state_to_slabs(s_re, s_im, n_slabs):
    n_batch, g, p = s_re.shape
    f = lambda s: s.astype(F32).reshape(n_batch, n_slabs, (g // n_slabs) * p).swapaxes(0, 1)
    return jnp.concatenate([f(s_re), f(s_im)], axis=-1)


def _slabs_to_state(h, g, p):
    n_slabs, n_batch, st = h.shape
    half = st // 2
    f = lambda s: s.swapaxes(0, 1).reshape(n_batch, g, p)
    return f(h[..., :half]), f(h[..., half:])


def kernel(x_prompt, x_sample, cache_ckv, cache_krope, state_s5_re, state_s5_im, norm_mix, w_in, norm_q, w_uq, norm_kv, w_uk, w_uv, s5_a_re, s5_a_im, s5_log_dt, s5_b_re, s5_b_im, s5_c_re, s5_c_im, s5_d, w_glu_a, w_glu_b, w_o, norm_mlp, w_up, w_down, norm_final):
    depth = w_in.shape[0]
    d = x_prompt.shape[-1]
    q_lora, kv_lora = norm_q.shape[1], norm_kv.shape[1]
    n_heads, nope = w_uk.shape[2], w_uk.shape[3]
    v_dim = w_uv.shape[3]
    rope = cache_krope.shape[-1]
    n_groups, n_state, grp = s5_b_re.shape[1:]
    width = n_groups * grp
    past_len = cache_ckv.shape[2]
    sdt = state_s5_re.dtype
    assert nope == LANES and v_dim == LANES and 2 * rope == LANES and d % (4 * LANES) == 0
    assert width % LANES == 0 and LANES % grp == 0 and x_prompt.shape[0] == SUBLANES == x_sample.shape[0]
    n_slabs = width // LANES
    gps = n_groups // n_slabs
    half = gps * n_state
    scale = 1.0 / math.sqrt(nope + rope)
    c1 = q_lora + kv_lora
    c2 = c1 + rope
    c3 = c2 + width
    c4 = c3 + d

    hp, hs = x_prompt, x_sample
    outs_p, outs_s = [], []
    for l in range(depth):
        wi = w_in[l]
        w_kr = wi[:, c1:c2]
        wuq3 = w_uq[l].reshape(q_lora, n_heads, nope + rope)
        wuq_arr = jnp.concatenate([wuq3[..., :nope], wuq3[..., nope:], _swap_halves(wuq3[..., nope:])], axis=-1)
        w = dict(
            n_heads=n_heads, q_lora=q_lora, kv_lora=kv_lora, rope=rope, scale=scale,
            gmix=norm_mix[l][None].astype(F32), gq=norm_q[l][None].astype(F32), gkv=norm_kv[l][None].astype(F32),
            wlat=jnp.concatenate([wi[:, :c1], w_kr, _swap_halves(w_kr)], axis=1).astype(BF16),
            wuq=wuq_arr.reshape(q_lora, n_heads * 2 * LANES).astype(BF16),
            wu=wi[:, c2:c3].astype(BF16), wga=wi[:, c3:c4].astype(BF16), wgb=wi[:, c4:].astype(BF16),
            wuk=w_uk[l].reshape(kv_lora, n_heads * nope).astype(BF16),
            wuv=w_uv[l].reshape(kv_lora, n_heads * v_dim).astype(BF16),
            wglu_a=w_glu_a[l].astype(BF16), wglu_b=w_glu_b[l].astype(BF16), wo=w_o[l].astype(BF16),
        )
        abr, abi, bbr, bbi = _s5_discretize(s5_a_re[l], s5_a_im[l], s5_log_dt[l], s5_b_re[l], s5_b_im[l])
        eye = jnp.eye(gps, dtype=F32)
        bd_in = lambda b: jnp.einsum("jgpm,gh->jgmhp", b.reshape(n_slabs, gps, n_state, grp), eye).reshape(
            n_slabs, LANES, half)
        bd_out = lambda c: jnp.einsum("jgmp,gh->jgphm", c.astype(F32).reshape(n_slabs, gps, grp, n_state), eye).reshape(
            n_slabs, half, LANES)
        s5w = dict(
            a=jnp.concatenate([abr.reshape(n_slabs, 1, half), abi.reshape(n_slabs, 1, half)], axis=-1),
            wb=jnp.concatenate([bd_in(bbr), bd_in(bbi)], axis=-1).astype(BF16),
            wc=jnp.concatenate([bd_out(s5_c_re[l]), -bd_out(s5_c_im[l])], axis=1).astype(BF16),
            d=s5_d[l].astype(F32).reshape(n_slabs, 1, LANES),
        )
        gmlp, gfin = norm_mlp[l][None].astype(F32), norm_final[None].astype(F32)
        wup, wdn = w_up[l].astype(BF16), w_down[l].astype(BF16)

        bp, lp, _ = hp.shape
        h0 = jnp.zeros((n_slabs, bp, 2 * half), F32)
        last = l == depth - 1
        h, ckv, kr, hfin = _group(hp, 0, w, s5w, h0, None, row_tile=_pick_tile(lp, 256, 16),
                                  attn_tile=_pick_tile(lp, 512, CHUNK), s5_tl=_pick_tile(lp, 64, SUBLANES))
        outs_p.append((ckv, kr) + _slabs_to_state(hfin, n_groups, n_state))
        hp = _mlp(h, gmlp, wup, wdn, gfin, tile=_pick_tile(bp * lp, 512, 16), fb=1024,
                  final_norm=last).reshape(bp, lp, d)

        bs, ls, _ = hs.shape
        h0 = _state_to_slabs(state_s5_re[l], state_s5_im[l], n_slabs)
        h, ckv, kr, hfin = _group(hs, past_len, w, s5w, h0, (cache_ckv[l], cache_krope[l]),
                                  row_tile=bs * ls, attn_tile=None, s5_tl=ls)
        outs_s.append((ckv, kr) + _slabs_to_state(hfin, n_groups, n_state))
        hs = _mlp(h, gmlp, wup, wdn, gfin, tile=bs * ls, fb=1024, final_norm=last).reshape(bs, ls, d)

    stack = lambda outs, i, dt: jnp.stack([o[i].astype(dt) for o in outs], axis=0)
    return (hp, hs,
            stack(outs_p, 0, F32), stack(outs_p, 1, F32), stack(outs_p, 2, sdt), stack(outs_p, 3, sdt),
            stack(outs_s, 0, F32), stack(outs_s, 1, F32), stack(outs_s, 2, sdt), stack(outs_s, 3, sdt))
```

```python
import functools
import math

import jax
import jax.numpy as jnp
from jax import lax
from jax.experimental import pallas as pl
from jax.experimental.pallas import tpu as pltpu

F32 = jnp.float32
BF16 = jnp.bfloat16

EPS = 1e-6
CHUNK = 64
ROPE_THETA = 10000.0
LANES = 128
SUBLANES = 8
VMEM_LIMIT = 56 * 1024 * 1024

PROJ_ROWS = 256
KV_ROWS = 1024
MERGE_ROWS = 512
MLP_ROWS = 512
FF_BLOCK = 1024
ATTN_TILE = 512
ATTN_CHAINS = 4
S5_TL = 64
S5_SLABS = 4


def _cparams(n_axes):
    return pltpu.CompilerParams(dimension_semantics=("arbitrary",) * n_axes,
                                vmem_limit_bytes=VMEM_LIMIT)


def _const_spec(shape):
    nd = len(shape)
    return pl.BlockSpec(shape, lambda *_: (0,) * nd, pipeline_mode=pl.Buffered(1))


def _rms(x):
    return x * lax.rsqrt(jnp.mean(x * x, axis=-1, keepdims=True) + EPS)


def _pick_tile(n, cap, mult):
    best = None
    for t in range(mult, min(n, cap) + 1, mult):
        if n % t == 0:
            best = t
    assert best is not None, (n, cap, mult)
    return best


def _disc_kernel(ar_ref, ai_ref, ldt_ref, br_ref, bi_ref, abr_ref, abi_ref, bbr_ref, bbi_ref):
    ar = ar_ref[...]
    ai = ai_ref[...]
    dt = jnp.exp(ldt_ref[...])
    mag = jnp.exp(ar * dt)
    abar_re = mag * jnp.cos(ai * dt)
    abar_im = mag * jnp.sin(ai * dt)
    den = ar * ar + ai * ai
    nr = abar_re - 1.0
    coef_re = (nr * ar + abar_im * ai) / den
    coef_im = (abar_im * ar - nr * ai) / den
    br = br_ref[...]
    bi = bi_ref[...]
    abr_ref[...] = abar_re
    abi_ref[...] = abar_im
    bbr_ref[...] = coef_re * br - coef_im * bi
    bbi_ref[...] = coef_re * bi + coef_im * br


def _s5_discretize(a_re, a_im, log_dt, b_re, b_im):
    g, p, m = b_re.shape
    rep = lambda a: jnp.repeat(a.astype(F32), m, axis=1)
    flat = lambda b: b.astype(F32).reshape(g, p * m)
    shp = jax.ShapeDtypeStruct((g, p * m), F32)
    abr, abi, bbr, bbi = pl.pallas_call(
        _disc_kernel, out_shape=(shp, shp, shp, shp), name="s5_disc",
    )(rep(a_re), rep(a_im), rep(log_dt), flat(b_re), flat(b_im))
    return abr[:, ::m], abi[:, ::m], bbr.reshape(g, p, m), bbi.reshape(g, p, m)


def _proj_kernel(x_ref, tab_ref, gmix_ref, wlat_ref, gq_ref, wuq_ref, gkv_ref, wu_ref,
                 q_ref, ckv_ref, kr_ref, u_ref, *, q_lora, kv_lora, n_heads, scale):
    x = x_ref[...]
    xn = (_rms(x) * gmix_ref[...]).astype(BF16)
    z = jnp.dot(xn, wlat_ref[...], preferred_element_type=F32)
    cqn = (_rms(z[:, :q_lora]) * gq_ref[...]).astype(BF16)
    tab = tab_ref[...]
    for h in range(n_heads):
        qh = jnp.dot(cqn, wuq_ref[:, h * 256:(h + 1) * 256], preferred_element_type=F32) * scale
        q_ref[h, :, 0:LANES] = qh[:, :LANES].astype(BF16)
        r = qh[:, LANES:] * tab
        q_ref[h, :, LANES:2 * LANES] = (r + pltpu.roll(r, 64, 1)).astype(BF16)
    ckv_ref[...] = _rms(z[:, q_lora:q_lora + kv_lora]) * gkv_ref[...]
    kr = z[:, q_lora + kv_lora:] * tab
    kr = kr + pltpu.roll(kr, 64, 1)
    kr_ref[...] = kr[:, :64]
    u_ref[...] = jnp.dot(xn, wu_ref[...], preferred_element_type=F32)


def _proj(x2, tab, tab_blocks, gmix, wlat, gq, wuq, gkv, wu, *, tile, n_heads, q_lora, kv_lora, rope, scale):
    t_rows, d = x2.shape
    n_tiles = t_rows // tile
    kern = functools.partial(_proj_kernel, q_lora=q_lora, kv_lora=kv_lora, n_heads=n_heads, scale=scale)
    return pl.pallas_call(
        kern,
        grid=(n_tiles,),
        in_specs=[
            pl.BlockSpec((tile, d), lambda r: (r, 0)),
            pl.BlockSpec((tile, LANES), lambda r: (r % tab_blocks, 0)),
            _const_spec(gmix.shape), _const_spec(wlat.shape), _const_spec(gq.shape),
            _const_spec(wuq.shape), _const_spec(gkv.shape), _const_spec(wu.shape),
        ],
        out_specs=[
            pl.BlockSpec((n_heads, tile, 2 * LANES), lambda r: (0, r, 0)),
            pl.BlockSpec((tile, kv_lora), lambda r: (r, 0)),
            pl.BlockSpec((tile, rope), lambda r: (r, 0)),
            pl.BlockSpec((tile, wu.shape[1]), lambda r: (r, 0)),
        ],
        out_shape=[
            jax.ShapeDtypeStruct((n_heads, t_rows, 2 * LANES), BF16),
            jax.ShapeDtypeStruct((t_rows, kv_lora), F32),
            jax.ShapeDtypeStruct((t_rows, rope), F32),
            jax.ShapeDtypeStruct((t_rows, wu.shape[1]), F32),
        ],
        compiler_params=_cparams(1),
        name="proj",
    )(x2, tab, gmix, wlat, gq, wuq, gkv, wu)


def _kvup_kernel(ckv_ref, kr_ref, wuk_ref, wuv_ref, k_ref, v_ref, *, n_heads):
    c = ckv_ref[...].astype(BF16)
    kr = kr_ref[...]
    krz = jnp.concatenate([kr, jnp.zeros_like(kr)], axis=1).astype(BF16)
    for hp in range(n_heads // 2):
        kn = jnp.dot(c, wuk_ref[:, hp * 256:(hp + 1) * 256], preferred_element_type=F32).astype(BF16)
        vv = jnp.dot(c, wuv_ref[:, hp * 256:(hp + 1) * 256], preferred_element_type=F32).astype(BF16)
        for s in range(2):
            h = 2 * hp + s
            k_ref[h, :, 0:LANES] = kn[:, s * LANES:(s + 1) * LANES]
            k_ref[h, :, LANES:2 * LANES] = krz
            v_ref[h] = vv[:, s * LANES:(s + 1) * LANES]


def _kvup(ckv2, kr2, wuk, wuv, *, tile, n_heads):
    t_rows, kv_lora = ckv2.shape
    rope = kr2.shape[1]
    kern = functools.partial(_kvup_kernel, n_heads=n_heads)
    return pl.pallas_call(
        kern,
        grid=(t_rows // tile,),
        in_specs=[
            pl.BlockSpec((tile, kv_lora), lambda r: (r, 0)),
            pl.BlockSpec((tile, rope), lambda r: (r, 0)),
            _const_spec(wuk.shape), _const_spec(wuv.shape),
        ],
        out_specs=[
            pl.BlockSpec((n_heads, tile, 2 * LANES), lambda r: (0, r, 0)),
            pl.BlockSpec((n_heads, tile, LANES), lambda r: (0, r, 0)),
        ],
        out_shape=[
            jax.ShapeDtypeStruct((n_heads, t_rows, 2 * LANES), BF16),
            jax.ShapeDtypeStruct((n_heads, t_rows, LANES), BF16),
        ],
        compiler_params=_cparams(1),
        name="kvup",
    )(ckv2, kr2, wuk, wuv)


def _attn_tile(q, k, v, m_ref, l_ref, acc_ref, mask):
    s = lax.dot_general(q, k, (((1,), (1,)), ((), ())), preferred_element_type=F32)
    if mask is not None:
        s = jnp.where(mask, s, -jnp.inf)
    n_full = s.shape[1] // LANES
    cols = [s[:, c * LANES:(c + 1) * LANES] for c in range(n_full)]
    if s.shape[1] % LANES:
        cols.append(jnp.concatenate(
            [s[:, n_full * LANES:], jnp.full((s.shape[0], LANES - s.shape[1] % LANES), -jnp.inf, F32)], axis=1))
    m_prev = m_ref[...]
    m_new = jnp.maximum(m_prev, jnp.max(functools.reduce(jnp.maximum, cols), axis=-1, keepdims=True))
    alpha = jnp.exp2(m_prev - m_new)
    ps = [jnp.exp2(c - m_new) for c in cols]
    l_ref[...] = alpha * l_ref[...] + functools.reduce(jnp.add, ps)
    p = jnp.concatenate([x.astype(BF16) for x in ps], axis=1)[:, :s.shape[1]]
    acc_ref[...] = alpha * acc_ref[...] + jnp.dot(p, v, preferred_element_type=F32)
    m_ref[...] = m_new


def _attn_init(m_ref, l_ref, acc_ref):
    m_ref[...] = jnp.full(m_ref.shape, -jnp.inf, F32)
    l_ref[...] = jnp.zeros(l_ref.shape, F32)
    acc_ref[...] = jnp.zeros(acc_ref.shape, F32)


def _attn_finish(l_ref, acc_ref):
    return acc_ref[...] / jnp.sum(l_ref[...], axis=-1, keepdims=True)


def _chunk_mask(tq, tk, q_off, k_off):
    qc = (lax.broadcasted_iota(jnp.int32, (tq, tk), 0) + q_off) // CHUNK
    kc = (lax.broadcasted_iota(jnp.int32, (tq, tk), 1) + k_off) // CHUNK
    return kc <= qc


def _attn_causal_kernel(q_ref, k_ref, v_ref, o_ref, *stat_refs, tile, n_chains):
    qi = pl.program_id(2)
    ms, ls, accs = stat_refs[0::3], stat_refs[1::3], stat_refs[2::3]
    for c in range(n_chains):
        _attn_init(ms[c], ls[c], accs[c])
    qs = [q_ref[c * tile:(c + 1) * tile, :] for c in range(n_chains)]

    def full_tile(j, carry):
        off = pl.multiple_of(j * tile, tile)
        k = k_ref[pl.ds(off, tile), :]
        v = v_ref[pl.ds(off, tile), :]
        for c in range(n_chains):
            _attn_tile(qs[c], k, v, ms[c], ls[c], accs[c], None)
        return carry

    lax.fori_loop(0, qi * n_chains, full_tile, 0)
    for d in range(n_chains):
        off = pl.multiple_of((qi * n_chains + d) * tile, tile)
        k = k_ref[pl.ds(off, tile), :]
        v = v_ref[pl.ds(off, tile), :]
        for c in range(d, n_chains):
            _attn_tile(qs[c], k, v, ms[c], ls[c], accs[c], _chunk_mask(tile, tile, 0, 0) if c == d else None)
    for c in range(n_chains):
        o_ref[c * tile:(c + 1) * tile, :] = _attn_finish(ls[c], accs[c]).astype(o_ref.dtype)


def _attn_single_kernel(q_ref, k_ref, v_ref, o_ref, m_ref, l_ref, acc_ref, *, q_pos0):
    tq, tk = q_ref.shape[0], k_ref.shape[0]
    _attn_init(m_ref, l_ref, acc_ref)
    _attn_tile(q_ref[...], k_ref[...], v_ref[...], m_ref, l_ref, acc_ref, _chunk_mask(tq, tk, q_pos0, 0))
    o_ref[...] = _attn_finish(l_ref, acc_ref).astype(o_ref.dtype)


def _attention(q, k, v, *, n_batch, lq, lk, tile, n_chains, q_pos0):
    n_heads = q.shape[0]
    causal = tile is not None
    tq = tile * n_chains if causal else lq
    nq = lq // tq
    if causal:
        assert lq == lk and q_pos0 == 0 and tile % CHUNK == 0 and lq % tq == 0
        kern = functools.partial(_attn_causal_kernel, tile=tile, n_chains=n_chains)
        rows = tile
    else:
        kern = functools.partial(_attn_single_kernel, q_pos0=q_pos0)
        n_chains, rows = 1, lq
    stats = [pltpu.VMEM((rows, LANES), F32) for _ in range(3 * n_chains)]
    return pl.pallas_call(
        kern,
        grid=(n_batch, n_heads, nq),
        in_specs=[
            pl.BlockSpec((None, tq, 2 * LANES), lambda b, h, i: (h, b * nq + i, 0)),
            pl.BlockSpec((None, lk, 2 * LANES), lambda b, h, i: (h, b, 0)),
            pl.BlockSpec((None, lk, LANES), lambda b, h, i: (h, b, 0)),
        ],
        out_specs=pl.BlockSpec((None, tq, LANES), lambda b, h, i: (h, b * nq + i, 0)),
        out_shape=jax.ShapeDtypeStruct((n_heads, n_batch * lq, LANES), BF16),
        scratch_shapes=stats,
        compiler_params=_cparams(3),
        name="attn_causal" if causal else "attn_single",
    )(q, k, v)


def _s5_kernel(u_ref, h0_ref, a_ref, wb_ref, wc_ref, d_ref, g_ref, hout_ref, *scratch, tl, n_batch, ns):
    i = pl.program_id(1)
    half = a_ref.shape[-1] // 2
    rows = tl * n_batch
    x_refs, hs_ref = scratch[:ns], scratch[ns]

    @pl.when(i == 0)
    def _():
        hs_ref[...] = h0_ref[...]

    ut = jnp.swapaxes(u_ref[...], 0, 1).reshape(rows, ns * LANES)
    for s in range(ns):
        x_ref = x_refs[s]
        u = ut[:, s * LANES:(s + 1) * LANES]
        ub = u.astype(BF16)
        x_ref[:, :half] = jnp.dot(ub, wb_ref[s, :, :half], preferred_element_type=F32)
        x_ref[:, half:] = jnp.dot(ub, wb_ref[s, :, half:], preferred_element_type=F32)
        ar = jnp.broadcast_to(a_ref[s, :, :half], (n_batch, half))
        ai = jnp.broadcast_to(a_ref[s, :, half:], (n_batch, half))
        hr, hi = hs_ref[s, :, :half], hs_ref[s, :, half:]
        for t in range(tl):
            r0 = t * n_batch
            nr = ar * hr - ai * hi + x_ref[r0:r0 + n_batch, :half]
            ni = ar * hi + ai * hr + x_ref[r0:r0 + n_batch, half:]
            x_ref[r0:r0 + n_batch, :half] = nr
            x_ref[r0:r0 + n_batch, half:] = ni
            hr, hi = nr, ni
        hs_ref[s, :, :half] = hr
        hs_ref[s, :, half:] = hi

        y = (jnp.dot(x_ref[:, :half].astype(BF16), wc_ref[s, :half, :], preferred_element_type=F32)
             + jnp.dot(x_ref[:, half:].astype(BF16), wc_ref[s, half:, :], preferred_element_type=F32))
        g = jax.nn.gelu(y + d_ref[s] * u)
        g_ref[:, :, s * LANES:(s + 1) * LANES] = jnp.swapaxes(
            g.reshape(tl, n_batch, LANES), 0, 1).astype(g_ref.dtype)

    @pl.when(i == pl.num_programs(1) - 1)
    def _():
        hout_ref[...] = hs_ref[...]


def _s5(u3, h0, a_l, wb, wc, d_l, *, tl, ns):
    n_batch, seq, width = u3.shape
    n_slabs = width // LANES
    st = a_l.shape[-1]
    kern = functools.partial(_s5_kernel, tl=tl, n_batch=n_batch, ns=ns)
    return pl.pallas_call(
        kern,
        grid=(n_slabs // ns, seq // tl),
        in_specs=[
            pl.BlockSpec((n_batch, tl, ns * LANES), lambda j, i: (0, i, j)),
            pl.BlockSpec((ns, n_batch, st), lambda j, i: (j, 0, 0)),
            pl.BlockSpec((ns, 1, st), lambda j, i: (j, 0, 0)),
            pl.BlockSpec((ns, LANES, st), lambda j, i: (j, 0, 0)),
            pl.BlockSpec((ns, st, LANES), lambda j, i: (j, 0, 0)),
            pl.BlockSpec((ns, 1, LANES), lambda j, i: (j, 0, 0)),
        ],
        out_specs=[
            pl.BlockSpec((n_batch, tl, ns * LANES), lambda j, i: (0, i, j)),
            pl.BlockSpec((ns, n_batch, st), lambda j, i: (j, 0, 0)),
        ],
        out_shape=[
            jax.ShapeDtypeStruct((n_batch, seq, width), BF16),
            jax.ShapeDtypeStruct((n_slabs, n_batch, st), F32),
        ],
        scratch_shapes=[pltpu.VMEM((tl * n_batch, st), F32) for _ in range(ns)] + [pltpu.VMEM((ns, n_batch, st), F32)],
        compiler_params=_cparams(2),
        name="s5",
    )(u3, h0, a_l, wb, wc, d_l)


def _merge_kernel(x_ref, g_ref, attn_ref, gmix_ref, wga_ref, wgb_ref, wa_ref, wb_ref, wo_ref,
                  h_ref, xn_ref, acc_ref):
    j = pl.program_id(1)

    @pl.when(j == 0)
    def _():
        xn_ref[...] = (_rms(x_ref[...]) * gmix_ref[...]).astype(BF16)
        acc_ref[...] = jnp.zeros(acc_ref.shape, F32)

    xn = xn_ref[...]
    g = g_ref[...]
    ga = jax.nn.sigmoid(jnp.dot(xn, wga_ref[...], preferred_element_type=F32))
    gb = jax.nn.sigmoid(jnp.dot(xn, wgb_ref[...], preferred_element_type=F32))
    s5o = (jnp.dot(g, wa_ref[...], preferred_element_type=F32)
           * jax.nn.sigmoid(jnp.dot(g, wb_ref[...], preferred_element_type=F32)))
    attn = jnp.concatenate([attn_ref[h].astype(F32) for h in range(attn_ref.shape[0])], axis=1)
    mix = (ga * attn + gb * s5o).astype(BF16)
    acc_ref[...] += jnp.dot(mix, wo_ref[...], preferred_element_type=F32)

    @pl.when(j == pl.num_programs(1) - 1)
    def _():
        h_ref[...] = x_ref[...] + acc_ref[...]


def _merge(x2, g2, attn, gmix, wga, wgb, wa, wb, wo, *, tile, cb):
    t_rows, d = x2.shape
    hpb = cb // LANES
    return pl.pallas_call(
        _merge_kernel,
        grid=(t_rows // tile, d // cb),
        in_specs=[
            pl.BlockSpec((tile, d), lambda r, j: (r, 0)),
            pl.BlockSpec((tile, d), lambda r, j: (r, 0)),
            pl.BlockSpec((hpb, tile, LANES), lambda r, j: (j, r, 0)),
            _const_spec(gmix.shape),
            pl.BlockSpec((d, cb), lambda r, j: (0, j)),
            pl.BlockSpec((d, cb), lambda r, j: (0, j)),
            pl.BlockSpec((d, cb), lambda r, j: (0, j)),
            pl.BlockSpec((d, cb), lambda r, j: (0, j)),
            pl.BlockSpec((cb, d), lambda r, j: (j, 0)),
        ],
        out_specs=pl.BlockSpec((tile, d), lambda r, j: (r, 0)),
        out_shape=jax.ShapeDtypeStruct((t_rows, d), F32),
        scratch_shapes=[pltpu.VMEM((tile, d), BF16), pltpu.VMEM((tile, d), F32)],
        compiler_params=_cparams(2),
        name="merge",
    )(x2, g2, attn, gmix, wga, wgb, wa, wb, wo)


def _mlp_kernel(h_ref, gmlp_ref, wup_ref, wdn_ref, gfin_ref, y_ref, hn_ref, acc_ref, *, final_norm):
    j = pl.program_id(1)

    @pl.when(j == 0)
    def _():
        hn_ref[...] = (_rms(h_ref[...]) * gmlp_ref[...]).astype(BF16)
        acc_ref[...] = jnp.zeros(acc_ref.shape, F32)

    a = jnp.maximum(jnp.dot(hn_ref[...], wup_ref[...], preferred_element_type=F32), 0.0)
    acc_ref[...] += jnp.dot((a * a).astype(BF16), wdn_ref[...], preferred_element_type=F32)

    @pl.when(j == pl.num_programs(1) - 1)
    def _():
        y = h_ref[...] + acc_ref[...]
        y_ref[...] = _rms(y) * gfin_ref[...] if final_norm else y


def _mlp(h2, gmlp, wup, wdn, gfin, *, tile, fb, final_norm):
    t_rows, d = h2.shape
    d_ff = wup.shape[1]
    return pl.pallas_call(
        functools.partial(_mlp_kernel, final_norm=final_norm),
        grid=(t_rows // tile, d_ff // fb),
        in_specs=[
            pl.BlockSpec((tile, d), lambda r, j: (r, 0)),
            _const_spec(gmlp.shape),
            pl.BlockSpec((d, fb), lambda r, j: (0, j)),
            pl.BlockSpec((fb, d), lambda r, j: (j, 0)),
            _const_spec(gfin.shape),
        ],
        out_specs=pl.BlockSpec((tile, d), lambda r, j: (r, 0)),
        out_shape=jax.ShapeDtypeStruct((t_rows, d), F32),
        scratch_shapes=[pltpu.VMEM((tile, d), BF16), pltpu.VMEM((tile, d), F32)],
        compiler_params=_cparams(2),
        name="mlp",
    )(h2, gmlp, wup, wdn, gfin)


def _rope_table(pos, rope):
    half = rope // 2
    inv = ROPE_THETA ** (-jnp.arange(half, dtype=F32) / half)
    ang = pos.astype(F32)[:, None] * inv[None, :]
    cos, sin = jnp.cos(ang), jnp.sin(ang)
    return jnp.concatenate([cos, cos, -sin, sin], axis=1)


def _swap_halves(w):
    half = w.shape[-1] // 2
    return jnp.concatenate([w[..., half:], w[..., :half]], axis=-1)


def _tiles(n_batch, seq, causal):
    rows = n_batch * seq
    t = dict(
        proj=_pick_tile(seq, PROJ_ROWS, 16) if seq >= PROJ_ROWS else rows,
        merge=_pick_tile(rows, MERGE_ROWS, 16),
        mlp=_pick_tile(rows, MLP_ROWS, 16),
        s5_tl=_pick_tile(seq, S5_TL, SUBLANES),
        attn=None, chains=1,
    )
    if causal:
        t["attn"] = _pick_tile(seq, ATTN_TILE, CHUNK)
        t["chains"] = max(c for c in range(1, ATTN_CHAINS + 1) if (seq // t["attn"]) % c == 0)
    return t


def _group(x, pos0, w, s5w, h0, cache):
    n_batch, seq, d = x.shape
    t_rows = n_batch * seq
    n_heads, kv_lora, rope = w["n_heads"], w["kv_lora"], w["rope"]
    x2 = x.reshape(t_rows, d)
    tiles = _tiles(n_batch, seq, causal=cache is None)
    row_tile = tiles["proj"]

    pos = pos0 + jnp.arange(seq)
    tab = _rope_table(pos, rope)
    if row_tile > seq:
        assert row_tile % seq == 0
        tab = jnp.tile(tab, (row_tile // seq, 1))
        tab_blocks = 1
    else:
        assert seq % row_tile == 0
        tab_blocks = seq // row_tile

    q, ckv, kr, u = _proj(x2, tab, tab_blocks, w["gmix"], w["wlat"], w["gq"], w["wuq"], w["gkv"], w["wu"],
                          tile=row_tile, n_heads=n_heads, q_lora=w["q_lora"], kv_lora=kv_lora, rope=rope,
                          scale=w["scale"])
    ckv3 = ckv.reshape(n_batch, seq, kv_lora)
    kr3 = kr.reshape(n_batch, seq, rope)
    if cache is not None:
        ckv_all = jnp.concatenate([cache[0].astype(F32), ckv3], axis=1)
        kr_all = jnp.concatenate([cache[1].astype(F32), kr3], axis=1)
    else:
        ckv_all, kr_all = ckv3, kr3
    lk = ckv_all.shape[1]
    k_rows = n_batch * lk
    k, v = _kvup(ckv_all.reshape(k_rows, kv_lora), kr_all.reshape(k_rows, rope), w["wuk"], w["wuv"],
                 tile=_pick_tile(k_rows, KV_ROWS, 16), n_heads=n_heads)
    attn = _attention(q, k, v, n_batch=n_batch, lq=seq, lk=lk, tile=tiles["attn"], n_chains=tiles["chains"],
                      q_pos0=pos0)

    n_slabs = s5w["a"].shape[0]
    g, h_fin = _s5(u.reshape(n_batch, seq, -1), h0, s5w["a"], s5w["wb"], s5w["wc"], s5w["d"], tl=tiles["s5_tl"],
                   ns=math.gcd(n_slabs, S5_SLABS))

    h = _merge(x2, g.reshape(t_rows, -1), attn, w["gmix"], w["wga"], w["wgb"], w["wglu_a"], w["wglu_b"], w["wo"],
               tile=tiles["merge"], cb=4 * LANES)
    return h, ckv3, kr3, h_fin, tiles["mlp"]


def _state_to_slabs(s_re, s_im, n_slabs):
    n_batch, g, p = s_re.shape
    f = lambda s: s.astype(F32).reshape(n_batch, n_slabs, (g // n_slabs) * p).swapaxes(0, 1)
    return jnp.concatenate([f(s_re), f(s_im)], axis=-1)


def _slabs_to_state(h, g, p):
    n_slabs, n_batch, st = h.shape
    half = st // 2
    f = lambda s: s.swapaxes(0, 1).reshape(n_batch, g, p)
    return f(h[..., :half]), f(h[..., half:])


def kernel(x_prompt, x_sample, cache_ckv, cache_krope, state_s5_re, state_s5_im, norm_mix, w_in, norm_q, w_uq, norm_kv, w_uk, w_uv, s5_a_re, s5_a_im, s5_log_dt, s5_b_re, s5_b_im, s5_c_re, s5_c_im, s5_d, w_glu_a, w_glu_b, w_o, norm_mlp, w_up, w_down, norm_final):
    depth = w_in.shape[0]
    d = x_prompt.shape[-1]
    q_lora, kv_lora = norm_q.shape[1], norm_kv.shape[1]
    n_heads, nope = w_uk.shape[2], w_uk.shape[3]
    v_dim = w_uv.shape[3]
    rope = cache_krope.shape[-1]
    n_groups, n_state, grp = s5_b_re.shape[1:]
    width = n_groups * grp
    past_len = cache_ckv.shape[2]
    sdt = state_s5_re.dtype
    assert nope == LANES and v_dim == LANES and 2 * rope == LANES and d % (4 * LANES) == 0
    assert width % LANES == 0 and LANES % grp == 0 and x_prompt.shape[0] == SUBLANES == x_sample.shape[0]
    n_slabs = width // LANES
    gps = n_groups // n_slabs
    half = gps * n_state
    scale = math.log2(math.e) / math.sqrt(nope + rope)
    c1 = q_lora + kv_lora
    c2 = c1 + rope
    c3 = c2 + width
    c4 = c3 + d

    hp, hs = x_prompt, x_sample
    outs_p, outs_s = [], []
    for l in range(depth):
        wi = w_in[l]
        w_kr = wi[:, c1:c2]
        wuq3 = w_uq[l].reshape(q_lora, n_heads, nope + rope)
        wuq_arr = jnp.concatenate([wuq3[..., :nope], wuq3[..., nope:], _swap_halves(wuq3[..., nope:])], axis=-1)
        w = dict(
            n_heads=n_heads, q_lora=q_lora, kv_lora=kv_lora, rope=rope, scale=scale,
            gmix=norm_mix[l][None].astype(F32), gq=norm_q[l][None].astype(F32), gkv=norm_kv[l][None].astype(F32),
            wlat=jnp.concatenate([wi[:, :c1], w_kr, _swap_halves(w_kr)], axis=1).astype(BF16),
            wuq=wuq_arr.reshape(q_lora, n_heads * 2 * LANES).astype(BF16),
            wu=wi[:, c2:c3].astype(BF16), wga=wi[:, c3:c4].astype(BF16), wgb=wi[:, c4:].astype(BF16),
            wuk=w_uk[l].reshape(kv_lora, n_heads * nope).astype(BF16),
            wuv=w_uv[l].reshape(kv_lora, n_heads * v_dim).astype(BF16),
            wglu_a=w_glu_a[l].astype(BF16), wglu_b=w_glu_b[l].astype(BF16), wo=w_o[l].astype(BF16),
        )
        abr, abi, bbr, bbi = _s5_discretize(s5_a_re[l], s5_a_im[l], s5_log_dt[l], s5_b_re[l], s5_b_im[l])
        eye = jnp.eye(gps, dtype=F32)
        bd_in = lambda b: jnp.einsum("jgpm,gh->jgmhp", b.reshape(n_slabs, gps, n_state, grp), eye).reshape(
            n_slabs, LANES, half)
        bd_out = lambda c: jnp.einsum("jgmp,gh->jgphm", c.astype(F32).reshape(n_slabs, gps, grp, n_state), eye).reshape(
            n_slabs, half, LANES)
        s5w = dict(
            a=jnp.concatenate([abr.reshape(n_slabs, 1, half), abi.reshape(n_slabs, 1, half)], axis=-1),
            wb=jnp.concatenate([bd_in(bbr), bd_in(bbi)], axis=-1).astype(BF16),
            wc=jnp.concatenate([bd_out(s5_c_re[l]), -bd_out(s5_c_im[l])], axis=1).astype(BF16),
            d=s5_d[l].astype(F32).reshape(n_slabs, 1, LANES),
        )
        gmlp, gfin = norm_mlp[l][None].astype(F32), norm_final[None].astype(F32)
        wup, wdn = w_up[l].astype(BF16), w_down[l].astype(BF16)

        bp, lp, _ = hp.shape
        h0 = jnp.zeros((n_slabs, bp, 2 * half), F32)
        last = l == depth - 1
        h, ckv, kr, hfin, mlp_tile = _group(hp, 0, w, s5w, h0, None)
        outs_p.append((ckv, kr) + _slabs_to_state(hfin, n_groups, n_state))
        hp = _mlp(h, gmlp, wup, wdn, gfin, tile=mlp_tile, fb=FF_BLOCK, final_norm=last).reshape(bp, lp, d)

        bs, ls, _ = hs.shape
        h0 = _state_to_slabs(state_s5_re[l], state_s5_im[l], n_slabs)
        h, ckv, kr, hfin, mlp_tile = _group(hs, past_len, w, s5w, h0, (cache_ckv[l], cache_krope[l]))
        outs_s.append((ckv, kr) + _slabs_to_state(hfin, n_groups, n_state))
        hs = _mlp(h, gmlp, wup, wdn, gfin, tile=mlp_tile, fb=FF_BLOCK, final_norm=last).reshape(bs, ls, d)

    stack = lambda outs, i, dt: jnp.stack([o[i].astype(dt) for o in outs], axis=0)
    return (hp, hs,
            stack(outs_p, 0, F32), stack(outs_p, 1, F32), stack(outs_p, 2, sdt), stack(outs_p, 3, sdt),
            stack(outs_s, 0, F32), stack(outs_s, 1, F32), stack(outs_s, 2, sdt), stack(outs_s, 3, sdt))
```

```python
import functools
import math

import jax
import jax.numpy as jnp
from jax import lax
from jax.experimental import pallas as pl
from jax.experimental.pallas import tpu as pltpu

F32 = jnp.float32
BF16 = jnp.bfloat16

EPS = 1e-6
CHUNK = 64
ROPE_THETA = 10000.0
LANES = 128
SUBLANES = 8
VMEM_LIMIT = 56 * 1024 * 1024

PROJ_ROWS = 256
KV_ROWS = 1024
MERGE_ROWS = 512
MLP_ROWS = 512
FF_BLOCK = 1024
ATTN_TILE = 512
ATTN_CHAINS = 8
S5_TL = 128
S5_SLABS = 4


def _cparams(n_axes):
    return pltpu.CompilerParams(dimension_semantics=("arbitrary",) * n_axes,
                                vmem_limit_bytes=VMEM_LIMIT)


def _const_spec(shape):
    nd = len(shape)
    return pl.BlockSpec(shape, lambda *_: (0,) * nd, pipeline_mode=pl.Buffered(1))


def _rms(x):
    return x * lax.rsqrt(jnp.mean(x * x, axis=-1, keepdims=True) + EPS)


def _pick_tile(n, cap, mult):
    best = None
    for t in range(mult, min(n, cap) + 1, mult):
        if n % t == 0:
            best = t
    assert best is not None, (n, cap, mult)
    return best


def _disc_kernel(ar_ref, ai_ref, ldt_ref, br_ref, bi_ref, abr_ref, abi_ref, bbr_ref, bbi_ref):
    ar = ar_ref[...]
    ai = ai_ref[...]
    dt = jnp.exp(ldt_ref[...])
    mag = jnp.exp(ar * dt)
    abar_re = mag * jnp.cos(ai * dt)
    abar_im = mag * jnp.sin(ai * dt)
    den = ar * ar + ai * ai
    nr = abar_re - 1.0
    coef_re = (nr * ar + abar_im * ai) / den
    coef_im = (abar_im * ar - nr * ai) / den
    br = br_ref[...]
    bi = bi_ref[...]
    abr_ref[...] = abar_re
    abi_ref[...] = abar_im
    bbr_ref[...] = coef_re * br - coef_im * bi
    bbi_ref[...] = coef_re * bi + coef_im * br


def _s5_discretize(a_re, a_im, log_dt, b_re, b_im):
    g, p, m = b_re.shape
    rep = lambda a: jnp.repeat(a.astype(F32), m, axis=1)
    flat = lambda b: b.astype(F32).reshape(g, p * m)
    shp = jax.ShapeDtypeStruct((g, p * m), F32)
    abr, abi, bbr, bbi = pl.pallas_call(
        _disc_kernel, out_shape=(shp, shp, shp, shp), name="s5_disc",
    )(rep(a_re), rep(a_im), rep(log_dt), flat(b_re), flat(b_im))
    return abr[:, ::m], abi[:, ::m], bbr.reshape(g, p, m), bbi.reshape(g, p, m)


def _proj_kernel(x_ref, tab_ref, gmix_ref, wlat_ref, gq_ref, wuq_ref, gkv_ref, wu_ref,
                 q_ref, ckv_ref, kr_ref, u_ref, *, q_lora, kv_lora, n_heads, scale):
    x = x_ref[...]
    xn = (_rms(x) * gmix_ref[...]).astype(BF16)
    z = jnp.dot(xn, wlat_ref[...], preferred_element_type=F32)
    cqn = (_rms(z[:, :q_lora]) * gq_ref[...]).astype(BF16)
    tab = tab_ref[...]
    for h in range(n_heads):
        qh = jnp.dot(cqn, wuq_ref[:, h * 256:(h + 1) * 256], preferred_element_type=F32) * scale
        q_ref[h, :, 0:LANES] = qh[:, :LANES].astype(BF16)
        r = qh[:, LANES:] * tab
        q_ref[h, :, LANES:2 * LANES] = (r + pltpu.roll(r, 64, 1)).astype(BF16)
    ckv_ref[...] = _rms(z[:, q_lora:q_lora + kv_lora]) * gkv_ref[...]
    kr = z[:, q_lora + kv_lora:] * tab
    kr = kr + pltpu.roll(kr, 64, 1)
    kr_ref[...] = kr[:, :64]
    u_ref[...] = jnp.dot(xn, wu_ref[...], preferred_element_type=F32)


def _proj(x2, tab, tab_blocks, gmix, wlat, gq, wuq, gkv, wu, *, tile, n_heads, q_lora, kv_lora, rope, scale):
    t_rows, d = x2.shape
    n_tiles = t_rows // tile
    kern = functools.partial(_proj_kernel, q_lora=q_lora, kv_lora=kv_lora, n_heads=n_heads, scale=scale)
    return pl.pallas_call(
        kern,
        grid=(n_tiles,),
        in_specs=[
            pl.BlockSpec((tile, d), lambda r: (r, 0)),
            pl.BlockSpec((tile, LANES), lambda r: (r % tab_blocks, 0)),
            _const_spec(gmix.shape), _const_spec(wlat.shape), _const_spec(gq.shape),
            _const_spec(wuq.shape), _const_spec(gkv.shape), _const_spec(wu.shape),
        ],
        out_specs=[
            pl.BlockSpec((n_heads, tile, 2 * LANES), lambda r: (0, r, 0)),
            pl.BlockSpec((tile, kv_lora), lambda r: (r, 0)),
            pl.BlockSpec((tile, rope), lambda r: (r, 0)),
            pl.BlockSpec((tile, wu.shape[1]), lambda r: (r, 0)),
        ],
        out_shape=[
            jax.ShapeDtypeStruct((n_heads, t_rows, 2 * LANES), BF16),
            jax.ShapeDtypeStruct((t_rows, kv_lora), F32),
            jax.ShapeDtypeStruct((t_rows, rope), F32),
            jax.ShapeDtypeStruct((t_rows, wu.shape[1]), F32),
        ],
        compiler_params=_cparams(1),
        name="proj",
    )(x2, tab, gmix, wlat, gq, wuq, gkv, wu)


def _kvup_kernel(ckv_ref, kr_ref, wuk_ref, wuv_ref, k_ref, v_ref, *, n_heads):
    c = ckv_ref[...].astype(BF16)
    kr = kr_ref[...]
    krz = jnp.concatenate([kr, jnp.zeros_like(kr)], axis=1).astype(BF16)
    ones = jnp.ones((c.shape[0], LANES), BF16)
    for hp in range(n_heads // 2):
        kn = jnp.dot(c, wuk_ref[:, hp * 256:(hp + 1) * 256], preferred_element_type=F32).astype(BF16)
        vv = jnp.dot(c, wuv_ref[:, hp * 256:(hp + 1) * 256], preferred_element_type=F32).astype(BF16)
        for s in range(2):
            h = 2 * hp + s
            k_ref[h, :, 0:LANES] = kn[:, s * LANES:(s + 1) * LANES]
            k_ref[h, :, LANES:2 * LANES] = krz
            v_ref[h, :, 0:LANES] = vv[:, s * LANES:(s + 1) * LANES]
            v_ref[h, :, LANES:2 * LANES] = ones


def _kvup(ckv2, kr2, wuk, wuv, *, tile, n_heads):
    t_rows, kv_lora = ckv2.shape
    rope = kr2.shape[1]
    kern = functools.partial(_kvup_kernel, n_heads=n_heads)
    return pl.pallas_call(
        kern,
        grid=(t_rows // tile,),
        in_specs=[
            pl.BlockSpec((tile, kv_lora), lambda r: (r, 0)),
            pl.BlockSpec((tile, rope), lambda r: (r, 0)),
            _const_spec(wuk.shape), _const_spec(wuv.shape),
        ],
        out_specs=[
            pl.BlockSpec((n_heads, tile, 2 * LANES), lambda r: (0, r, 0)),
            pl.BlockSpec((n_heads, tile, 2 * LANES), lambda r: (0, r, 0)),
        ],
        out_shape=[
            jax.ShapeDtypeStruct((n_heads, t_rows, 2 * LANES), BF16),
            jax.ShapeDtypeStruct((n_heads, t_rows, 2 * LANES), BF16),
        ],
        compiler_params=_cparams(1),
        name="kvup",
    )(ckv2, kr2, wuk, wuv)


def _attn_tile(q, k, v1, m_ref, acc_ref, mask):
    s = lax.dot_general(q, k, (((1,), (1,)), ((), ())), preferred_element_type=F32)
    if mask is not None:
        s = jnp.where(mask, s, -jnp.inf)
    cols = [s[:, c * LANES:(c + 1) * LANES] for c in range(s.shape[1] // LANES)]
    m_prev = m_ref[...]
    m_new = jnp.maximum(m_prev, jnp.max(functools.reduce(jnp.maximum, cols), axis=-1, keepdims=True))
    alpha = jnp.exp2(m_prev - m_new)
    p = jnp.concatenate([jnp.exp2(c - m_new).astype(BF16) for c in cols], axis=1)
    acc_ref[...] = (jnp.concatenate([alpha, alpha], axis=1) * acc_ref[...]
                    + jnp.dot(p, v1, preferred_element_type=F32))
    m_ref[...] = m_new


def _attn_init(m_ref, acc_ref):
    m_ref[...] = jnp.full(m_ref.shape, -jnp.inf, F32)
    acc_ref[...] = jnp.zeros(acc_ref.shape, F32)


def _attn_finish(acc_ref):
    acc = acc_ref[...]
    return acc[:, :LANES] / acc[:, LANES:]


def _chunk_mask(tq, tk, q_off, k_off):
    qc = (lax.broadcasted_iota(jnp.int32, (tq, tk), 0) + q_off) // CHUNK
    kc = (lax.broadcasted_iota(jnp.int32, (tq, tk), 1) + k_off) // CHUNK
    return kc <= qc


def _attn_causal_kernel(q_ref, k_ref, v_ref, o_ref, *stat_refs, tile, n_chains, n_qblocks):
    ms, accs = stat_refs[0::2], stat_refs[1::2]
    for c in range(n_chains):
        _attn_init(ms[c], accs[c])
    qs = [q_ref[c * tile:(c + 1) * tile, :] for c in range(n_chains)]

    if n_qblocks > 1:
        qi = pl.program_id(2)

        def full_tile(j, carry):
            off = pl.multiple_of(j * tile, tile)
            k = k_ref[pl.ds(off, tile), :]
            v1 = v_ref[pl.ds(off, tile), :]
            for c in range(n_chains):
                _attn_tile(qs[c], k, v1, ms[c], accs[c], None)
            return carry

        lax.fori_loop(0, qi * n_chains, full_tile, 0)
        base = qi * n_chains
    else:
        base = 0
    for d in range(n_chains):
        off = (base + d) * tile
        if n_qblocks > 1:
            off = pl.multiple_of(off, tile)
        k = k_ref[pl.ds(off, tile), :]
        v1 = v_ref[pl.ds(off, tile), :]
        for c in range(d, n_chains):
            _attn_tile(qs[c], k, v1, ms[c], accs[c], _chunk_mask(tile, tile, 0, 0) if c == d else None)
    for c in range(n_chains):
        o_ref[c * tile:(c + 1) * tile, :] = _attn_finish(accs[c]).astype(o_ref.dtype)


def _attention(q, k, v1, *, n_batch, seq, tile, n_chains):
    n_heads = q.shape[0]
    tq = tile * n_chains
    nq = seq // tq
    assert tile % CHUNK == 0 and seq % tq == 0
    stats = []
    for _ in range(n_chains):
        stats += [pltpu.VMEM((tile, LANES), F32), pltpu.VMEM((tile, 2 * LANES), F32)]
    return pl.pallas_call(
        functools.partial(_attn_causal_kernel, tile=tile, n_chains=n_chains, n_qblocks=nq),
        grid=(n_batch, n_heads, nq),
        in_specs=[
            pl.BlockSpec((None, tq, 2 * LANES), lambda b, h, i: (h, b * nq + i, 0)),
            pl.BlockSpec((None, seq, 2 * LANES), lambda b, h, i: (h, b, 0)),
            pl.BlockSpec((None, seq, 2 * LANES), lambda b, h, i: (h, b, 0)),
        ],
        out_specs=pl.BlockSpec((None, tq, LANES), lambda b, h, i: (h, b * nq + i, 0)),
        out_shape=jax.ShapeDtypeStruct((n_heads, n_batch * seq, LANES), BF16),
        scratch_shapes=stats,
        compiler_params=_cparams(3),
        name="attn_causal",
    )(q, k, v1)


def _attn_latent_kernel(q_ref, cc_ref, cn_ref, kc_ref, kn_ref, wuk_ref, wuv_ref, o_ref, *, q_pos0, n_heads):
    ls = q_ref.shape[1]
    nt = (((1,), (1,)), ((), ()))
    ckv = jnp.concatenate([cc_ref[...], cn_ref[...]], axis=0).astype(BF16)
    kr = jnp.concatenate([kc_ref[...], kn_ref[...]], axis=0)
    krz = jnp.concatenate([kr, jnp.zeros_like(kr)], axis=1).astype(BF16)
    q_lat = jnp.concatenate(
        [lax.dot_general(q_ref[h, :, :LANES], wuk_ref[:, h * LANES:(h + 1) * LANES], nt, preferred_element_type=F32)
         for h in range(n_heads)], axis=0).astype(BF16)
    q_rope = jnp.concatenate([q_ref[h, :, LANES:] for h in range(n_heads)], axis=0)
    s = (lax.dot_general(q_lat, ckv, nt, preferred_element_type=F32)
         + lax.dot_general(q_rope, krz, nt, preferred_element_type=F32))
    rows, lk = s.shape
    qc = (lax.rem(lax.broadcasted_iota(jnp.int32, (rows, lk), 0), ls) + q_pos0) // CHUNK
    kc = lax.broadcasted_iota(jnp.int32, (rows, lk), 1) // CHUNK
    s = jnp.where(kc <= qc, s, -jnp.inf)
    p = jnp.exp2(s - jnp.max(s, axis=-1, keepdims=True))
    l = jnp.sum(p, axis=-1, keepdims=True)
    o_lat = (jnp.dot(p.astype(BF16), ckv, preferred_element_type=F32) / l).astype(BF16)
    for h in range(n_heads):
        o_ref[h] = jnp.dot(o_lat[h * ls:(h + 1) * ls, :], wuv_ref[:, h * LANES:(h + 1) * LANES],
                           preferred_element_type=F32).astype(o_ref.dtype)


def _attention_latent(q, ckv_cache, ckv_new, kr_cache, kr_new, wuk, wuv, *, q_pos0):
    n_heads = q.shape[0]
    n_batch, past, kv_lora = ckv_cache.shape
    ls, rope = ckv_new.shape[1], kr_new.shape[2]
    kern = functools.partial(_attn_latent_kernel, q_pos0=q_pos0, n_heads=n_heads)
    return pl.pallas_call(
        kern,
        grid=(n_batch,),
        in_specs=[
            pl.BlockSpec((n_heads, ls, 2 * LANES), lambda b: (0, b, 0)),
            pl.BlockSpec((None, past, kv_lora), lambda b: (b, 0, 0)),
            pl.BlockSpec((None, ls, kv_lora), lambda b: (b, 0, 0)),
            pl.BlockSpec((None, past, rope), lambda b: (b, 0, 0)),
            pl.BlockSpec((None, ls, rope), lambda b: (b, 0, 0)),
            _const_spec(wuk.shape), _const_spec(wuv.shape),
        ],
        out_specs=pl.BlockSpec((n_heads, ls, LANES), lambda b: (0, b, 0)),
        out_shape=jax.ShapeDtypeStruct((n_heads, n_batch * ls, LANES), BF16),
        compiler_params=_cparams(1),
        name="attn_latent",
    )(q, ckv_cache, ckv_new, kr_cache, kr_new, wuk, wuv)


def _s5_kernel(u_ref, h0_ref, a_ref, wb_ref, wc_ref, d_ref, g_ref, hout_ref, *scratch, tl, n_batch, ns):
    i = pl.program_id(1)
    half = a_ref.shape[-1] // 2
    rows = tl * n_batch
    x_refs, hs_ref = scratch[:ns], scratch[ns]

    @pl.when(i == 0)
    def _():
        hs_ref[...] = h0_ref[...]

    ut = jnp.swapaxes(u_ref[...], 0, 1).reshape(rows, ns * LANES)
    for s in range(ns):
        x_ref = x_refs[s]
        u = ut[:, s * LANES:(s + 1) * LANES]
        ub = u.astype(BF16)
        x_ref[:, :half] = jnp.dot(ub, wb_ref[s, :, :half], preferred_element_type=F32)
        x_ref[:, half:] = jnp.dot(ub, wb_ref[s, :, half:], preferred_element_type=F32)
        ar = jnp.broadcast_to(a_ref[s, :, :half], (n_batch, half))
        ai = jnp.broadcast_to(a_ref[s, :, half:], (n_batch, half))
        hr, hi = hs_ref[s, :, :half], hs_ref[s, :, half:]
        for t in range(tl):
            r0 = t * n_batch
            nr = ar * hr - ai * hi + x_ref[r0:r0 + n_batch, :half]
            ni = ar * hi + ai * hr + x_ref[r0:r0 + n_batch, half:]
            x_ref[r0:r0 + n_batch, :half] = nr
            x_ref[r0:r0 + n_batch, half:] = ni
            hr, hi = nr, ni
        hs_ref[s, :, :half] = hr
        hs_ref[s, :, half:] = hi

        y = (jnp.dot(x_ref[:, :half].astype(BF16), wc_ref[s, :half, :], preferred_element_type=F32)
             + jnp.dot(x_ref[:, half:].astype(BF16), wc_ref[s, half:, :], preferred_element_type=F32))
        g = jax.nn.gelu(y + d_ref[s] * u)
        g_ref[:, :, s * LANES:(s + 1) * LANES] = jnp.swapaxes(
            g.reshape(tl, n_batch, LANES), 0, 1).astype(g_ref.dtype)

    @pl.when(i == pl.num_programs(1) - 1)
    def _():
        hout_ref[...] = hs_ref[...]


def _s5(u3, h0, a_l, wb, wc, d_l, *, tl, ns):
    n_batch, seq, width = u3.shape
    n_slabs = width // LANES
    st = a_l.shape[-1]
    kern = functools.partial(_s5_kernel, tl=tl, n_batch=n_batch, ns=ns)
    return pl.pallas_call(
        kern,
        grid=(n_slabs // ns, seq // tl),
        in_specs=[
            pl.BlockSpec((n_batch, tl, ns * LANES), lambda j, i: (0, i, j)),
            pl.BlockSpec((ns, n_batch, st), lambda j, i: (j, 0, 0)),
            pl.BlockSpec((ns, 1, st), lambda j, i: (j, 0, 0)),
            pl.BlockSpec((ns, LANES, st), lambda j, i: (j, 0, 0)),
            pl.BlockSpec((ns, st, LANES), lambda j, i: (j, 0, 0)),
            pl.BlockSpec((ns, 1, LANES), lambda j, i: (j, 0, 0)),
        ],
        out_specs=[
            pl.BlockSpec((n_batch, tl, ns * LANES), lambda j, i: (0, i, j)),
            pl.BlockSpec((ns, n_batch, st), lambda j, i: (j, 0, 0)),
        ],
        out_shape=[
            jax.ShapeDtypeStruct((n_batch, seq, width), BF16),
            jax.ShapeDtypeStruct((n_slabs, n_batch, st), F32),
        ],
        scratch_shapes=[pltpu.VMEM((tl * n_batch, st), F32) for _ in range(ns)] + [pltpu.VMEM((ns, n_batch, st), F32)],
        compiler_params=_cparams(2),
        name="s5",
    )(u3, h0, a_l, wb, wc, d_l)


def _merge_kernel(x_ref, g_ref, attn_ref, gmix_ref, wga_ref, wgb_ref, wa_ref, wb_ref, wo_ref,
                  h_ref, xn_ref, acc_ref):
    j = pl.program_id(1)

    @pl.when(j == 0)
    def _():
        xn_ref[...] = (_rms(x_ref[...]) * gmix_ref[...]).astype(BF16)
        acc_ref[...] = jnp.zeros(acc_ref.shape, F32)

    xn = xn_ref[...]
    g = g_ref[...]
    ga = jax.nn.sigmoid(jnp.dot(xn, wga_ref[...], preferred_element_type=F32))
    gb = jax.nn.sigmoid(jnp.dot(xn, wgb_ref[...], preferred_element_type=F32))
    s5o = (jnp.dot(g, wa_ref[...], preferred_element_type=F32)
           * jax.nn.sigmoid(jnp.dot(g, wb_ref[...], preferred_element_type=F32)))
    attn = jnp.concatenate([attn_ref[h].astype(F32) for h in range(attn_ref.shape[0])], axis=1)
    mix = (ga * attn + gb * s5o).astype(BF16)
    acc_ref[...] += jnp.dot(mix, wo_ref[...], preferred_element_type=F32)

    @pl.when(j == pl.num_programs(1) - 1)
    def _():
        h_ref[...] = x_ref[...] + acc_ref[...]


def _merge(x2, g2, attn, gmix, wga, wgb, wa, wb, wo, *, tile, cb):
    t_rows, d = x2.shape
    hpb = cb // LANES
    return pl.pallas_call(
        _merge_kernel,
        grid=(t_rows // tile, d // cb),
        in_specs=[
            pl.BlockSpec((tile, d), lambda r, j: (r, 0)),
            pl.BlockSpec((tile, d), lambda r, j: (r, 0)),
            pl.BlockSpec((hpb, tile, LANES), lambda r, j: (j, r, 0)),
            _const_spec(gmix.shape),
            pl.BlockSpec((d, cb), lambda r, j: (0, j)),
            pl.BlockSpec((d, cb), lambda r, j: (0, j)),
            pl.BlockSpec((d, cb), lambda r, j: (0, j)),
            pl.BlockSpec((d, cb), lambda r, j: (0, j)),
            pl.BlockSpec((cb, d), lambda r, j: (j, 0)),
        ],
        out_specs=pl.BlockSpec((tile, d), lambda r, j: (r, 0)),
        out_shape=jax.ShapeDtypeStruct((t_rows, d), F32),
        scratch_shapes=[pltpu.VMEM((tile, d), BF16), pltpu.VMEM((tile, d), F32)],
        compiler_params=_cparams(2),
        name="merge",
    )(x2, g2, attn, gmix, wga, wgb, wa, wb, wo)


def _mlp_kernel(h_ref, gmlp_ref, wup_ref, wdn_ref, gfin_ref, y_ref, hn_ref, acc_ref, *, final_norm):
    j = pl.program_id(1)

    @pl.when(j == 0)
    def _():
        hn_ref[...] = (_rms(h_ref[...]) * gmlp_ref[...]).astype(BF16)
        acc_ref[...] = jnp.zeros(acc_ref.shape, F32)

    a = jnp.maximum(jnp.dot(hn_ref[...], wup_ref[...], preferred_element_type=F32), 0.0)
    acc_ref[...] += jnp.dot((a * a).astype(BF16), wdn_ref[...], preferred_element_type=F32)

    @pl.when(j == pl.num_programs(1) - 1)
    def _():
        y = h_ref[...] + acc_ref[...]
        y_ref[...] = _rms(y) * gfin_ref[...] if final_norm else y


def _mlp(h2, gmlp, wup, wdn, gfin, *, tile, fb, final_norm):
    t_rows, d = h2.shape
    d_ff = wup.shape[1]
    return pl.pallas_call(
        functools.partial(_mlp_kernel, final_norm=final_norm),
        grid=(t_rows // tile, d_ff // fb),
        in_specs=[
            pl.BlockSpec((tile, d), lambda r, j: (r, 0)),
            _const_spec(gmlp.shape),
            pl.BlockSpec((d, fb), lambda r, j: (0, j)),
            pl.BlockSpec((fb, d), lambda r, j: (j, 0)),
            _const_spec(gfin.shape),
        ],
        out_specs=pl.BlockSpec((tile, d), lambda r, j: (r, 0)),
        out_shape=jax.ShapeDtypeStruct((t_rows, d), F32),
        scratch_shapes=[pltpu.VMEM((tile, d), BF16), pltpu.VMEM((tile, d), F32)],
        compiler_params=_cparams(2),
        name="mlp",
    )(h2, gmlp, wup, wdn, gfin)


def _rope_table(pos, rope):
    half = rope // 2
    inv = ROPE_THETA ** (-jnp.arange(half, dtype=F32) / half)
    ang = pos.astype(F32)[:, None] * inv[None, :]
    cos, sin = jnp.cos(ang), jnp.sin(ang)
    return jnp.concatenate([cos, cos, -sin, sin], axis=1)


def _swap_halves(w):
    half = w.shape[-1] // 2
    return jnp.concatenate([w[..., half:], w[..., :half]], axis=-1)


def _tiles(n_batch, seq, causal):
    rows = n_batch * seq
    t = dict(
        proj=_pick_tile(seq, PROJ_ROWS, 16) if seq >= PROJ_ROWS else rows,
        merge=_pick_tile(rows, MERGE_ROWS, 16),
        mlp=_pick_tile(rows, MLP_ROWS, 16),
        s5_tl=_pick_tile(seq, S5_TL, SUBLANES),
        attn=None, chains=1,
    )
    if causal:
        t["attn"] = _pick_tile(seq, ATTN_TILE, CHUNK)
        t["chains"] = max(c for c in range(1, ATTN_CHAINS + 1) if (seq // t["attn"]) % c == 0)
    return t


def _group(x, pos0, w, s5w, h0, cache):
    n_batch, seq, d = x.shape
    t_rows = n_batch * seq
    n_heads, kv_lora, rope = w["n_heads"], w["kv_lora"], w["rope"]
    x2 = x.reshape(t_rows, d)
    tiles = _tiles(n_batch, seq, causal=cache is None)
    row_tile = tiles["proj"]

    pos = pos0 + jnp.arange(seq)
    tab = _rope_table(pos, rope)
    if row_tile > seq:
        assert row_tile % seq == 0
        tab = jnp.tile(tab, (row_tile // seq, 1))
        tab_blocks = 1
    else:
        assert seq % row_tile == 0
        tab_blocks = seq // row_tile

    q, ckv, kr, u = _proj(x2, tab, tab_blocks, w["gmix"], w["wlat"], w["gq"], w["wuq"], w["gkv"], w["wu"],
                          tile=row_tile, n_heads=n_heads, q_lora=w["q_lora"], kv_lora=kv_lora, rope=rope,
                          scale=w["scale"])
    ckv3 = ckv.reshape(n_batch, seq, kv_lora)
    kr3 = kr.reshape(n_batch, seq, rope)
    if cache is not None:
        attn = _attention_latent(q, cache[0].astype(F32), ckv3, cache[1].astype(F32), kr3, w["wuk"], w["wuv"],
                                 q_pos0=pos0)
    else:
        k, v1 = _kvup(ckv, kr, w["wuk"], w["wuv"], tile=_pick_tile(t_rows, KV_ROWS, 16), n_heads=n_heads)
        attn = _attention(q, k, v1, n_batch=n_batch, seq=seq, tile=tiles["attn"], n_chains=tiles["chains"])

    n_slabs = s5w["a"].shape[0]
    g, h_fin = _s5(u.reshape(n_batch, seq, -1), h0, s5w["a"], s5w["wb"], s5w["wc"], s5w["d"], tl=tiles["s5_tl"],
                   ns=math.gcd(n_slabs, S5_SLABS))

    h = _merge(x2, g.reshape(t_rows, -1), attn, w["gmix"], w["wga"], w["wgb"], w["wglu_a"], w["wglu_b"], w["wo"],
               tile=tiles["merge"], cb=4 * LANES)
    return h, ckv3, kr3, h_fin, tiles["mlp"]


def _state_to_slabs(s_re, s_im, n_slabs):
    n_batch, g, p = s_re.shape
    f = lambda s: s.astype(F32).reshape(n_batch, n_slabs, (g // n_slabs) * p).swapaxes(0, 1)
    return jnp.concatenate([f(s_re), f(s_im)], axis=-1)


def _slabs_to_state(h, g, p):
    n_slabs, n_batch, st = h.shape
    half = st // 2
    f = lambda s: s.swapaxes(0, 1).reshape(n_batch, g, p)
    return f(h[..., :half]), f(h[..., half:])


def kernel(x_prompt, x_sample, cache_ckv, cache_krope, state_s5_re, state_s5_im, norm_mix, w_in, norm_q, w_uq, norm_kv, w_uk, w_uv, s5_a_re, s5_a_im, s5_log_dt, s5_b_re, s5_b_im, s5_c_re, s5_c_im, s5_d, w_glu_a, w_glu_b, w_o, norm_mlp, w_up, w_down, norm_final):
    depth = w_in.shape[0]
    d = x_prompt.shape[-1]
    q_lora, kv_lora = norm_q.shape[1], norm_kv.shape[1]
    n_heads, nope = w_uk.shape[2], w_uk.shape[3]
    v_dim = w_uv.shape[3]
    rope = cache_krope.shape[-1]
    n_groups, n_state, grp = s5_b_re.shape[1:]
    width = n_groups * grp
    past_len = cache_ckv.shape[2]
    sdt = state_s5_re.dtype
    assert nope == LANES and v_dim == LANES and 2 * rope == LANES and d % (4 * LANES) == 0
    assert width % LANES == 0 and LANES % grp == 0 and x_prompt.shape[0] == SUBLANES == x_sample.shape[0]
    n_slabs = width // LANES
    gps = n_groups // n_slabs
    half = gps * n_state
    scale = math.log2(math.e) / math.sqrt(nope + rope)
    c1 = q_lora + kv_lora
    c2 = c1 + rope
    c3 = c2 + width
    c4 = c3 + d

    hp, hs = x_prompt, x_sample
    outs_p, outs_s = [], []
    for l in range(depth):
        wi = w_in[l]
        w_kr = wi[:, c1:c2]
        wuq3 = w_uq[l].reshape(q_lora, n_heads, nope + rope)
        wuq_arr = jnp.concatenate([wuq3[..., :nope], wuq3[..., nope:], _swap_halves(wuq3[..., nope:])], axis=-1)
        w = dict(
            n_heads=n_heads, q_lora=q_lora, kv_lora=kv_lora, rope=rope, scale=scale,
            gmix=norm_mix[l][None].astype(F32), gq=norm_q[l][None].astype(F32), gkv=norm_kv[l][None].astype(F32),
            wlat=jnp.concatenate([wi[:, :c1], w_kr, _swap_halves(w_kr)], axis=1).astype(BF16),
            wuq=wuq_arr.reshape(q_lora, n_heads * 2 * LANES).astype(BF16),
            wu=wi[:, c2:c3].astype(BF16), wga=wi[:, c3:c4].astype(BF16), wgb=wi[:, c4:].astype(BF16),
            wuk=w_uk[l].reshape(kv_lora, n_heads * nope).astype(BF16),
            wuv=w_uv[l].reshape(kv_lora, n_heads * v_dim).astype(BF16),
            wglu_a=w_glu_a[l].astype(BF16), wglu_b=w_glu_b[l].astype(BF16), wo=w_o[l].astype(BF16),
        )
        abr, abi, bbr, bbi = _s5_discretize(s5_a_re[l], s5_a_im[l], s5_log_dt[l], s5_b_re[l], s5_b_im[l])
        eye = jnp.eye(gps, dtype=F32)
        bd_in = lambda b: jnp.einsum("jgpm,gh->jgmhp", b.reshape(n_slabs, gps, n_state, grp), eye).reshape(
            n_slabs, LANES, half)
        bd_out = lambda c: jnp.einsum("jgmp,gh->jgphm", c.astype(F32).reshape(n_slabs, gps, grp, n_state), eye).reshape(
            n_slabs, half, LANES)
        s5w = dict(
            a=jnp.concatenate([abr.reshape(n_slabs, 1, half), abi.reshape(n_slabs, 1, half)], axis=-1),
            wb=jnp.concatenate([bd_in(bbr), bd_in(bbi)], axis=-1).astype(BF16),
            wc=jnp.concatenate([bd_out(s5_c_re[l]), -bd_out(s5_c_im[l])], axis=1).astype(BF16),
            d=s5_d[l].astype(F32).reshape(n_slabs, 1, LANES),
        )
        gmlp, gfin = norm_mlp[l][None].astype(F32), norm_final[None].astype(F32)
        wup, wdn = w_up[l].astype(BF16), w_down[l].astype(BF16)

        bp, lp, _ = hp.shape
        h0 = jnp.zeros((n_slabs, bp, 2 * half), F32)
        last = l == depth - 1
        h, ckv, kr, hfin, mlp_tile = _group(hp, 0, w, s5w, h0, None)
        outs_p.append((ckv, kr) + _slabs_to_state(hfin, n_groups, n_state))
        hp = _mlp(h, gmlp, wup, wdn, gfin, tile=mlp_tile, fb=FF_BLOCK, final_norm=last).reshape(bp, lp, d)

        bs, ls, _ = hs.shape
        h0 = _state_to_slabs(state_s5_re[l], state_s5_im[l], n_slabs)
        h, ckv, kr, hfin, mlp_tile = _group(hs, past_len, w, s5w, h0, (cache_ckv[l], cache_krope[l]))
        outs_s.append((ckv, kr) + _slabs_to_state(hfin, n_groups, n_state))
        hs = _mlp(h, gmlp, wup, wdn, gfin, tile=mlp_tile, fb=FF_BLOCK, final_norm=last).reshape(bs, ls, d)

    stack = lambda outs, i, dt: jnp.stack([o[i].astype(dt) for o in outs], axis=0)
    return (hp, hs,
            stack(outs_p, 0, F32), stack(outs_p, 1, F32), stack(outs_p, 2, sdt), stack(outs_p, 3, sdt),
            stack(outs_s, 0, F32), stack(outs_s, 1, F32), stack(outs_s, 2, sdt), stack(outs_s, 3, sdt))
```

```python
import functools
import math

import jax
import jax.numpy as jnp
from jax import lax
from jax.experimental import pallas as pl
from jax.experimental.pallas import tpu as pltpu

F32 = jnp.float32
BF16 = jnp.bfloat16

EPS = 1e-6
CHUNK = 64
ROPE_THETA = 10000.0
LANES = 128
SUBLANES = 8
VMEM_LIMIT = 56 * 1024 * 1024

PROJ_ROWS = 256
KV_ROWS = 1024
MERGE_ROWS = 512
MLP_ROWS = 512
FF_BLOCK = 1024
ATTN_TILE = 512
ATTN_CHAINS = 8
S5_TL = 128
S5_SLABS = 4


def _cparams(n_axes):
    return pltpu.CompilerParams(dimension_semantics=("arbitrary",) * n_axes,
                                vmem_limit_bytes=VMEM_LIMIT)


def _const_spec(shape):
    nd = len(shape)
    return pl.BlockSpec(shape, lambda *_: (0,) * nd, pipeline_mode=pl.Buffered(1))


def _rms(x):
    return x * lax.rsqrt(jnp.mean(x * x, axis=-1, keepdims=True) + EPS)


def _pick_tile(n, cap, mult):
    best = None
    for t in range(mult, min(n, cap) + 1, mult):
        if n % t == 0:
            best = t
    assert best is not None, (n, cap, mult)
    return best


def _disc_kernel(ar_ref, ai_ref, ldt_ref, br_ref, bi_ref, abr_ref, abi_ref, bbr_ref, bbi_ref):
    ar = ar_ref[...]
    ai = ai_ref[...]
    dt = jnp.exp(ldt_ref[...])
    mag = jnp.exp(ar * dt)
    abar_re = mag * jnp.cos(ai * dt)
    abar_im = mag * jnp.sin(ai * dt)
    den = ar * ar + ai * ai
    nr = abar_re - 1.0
    coef_re = (nr * ar + abar_im * ai) / den
    coef_im = (abar_im * ar - nr * ai) / den
    br = br_ref[...]
    bi = bi_ref[...]
    abr_ref[...] = abar_re
    abi_ref[...] = abar_im
    bbr_ref[...] = coef_re * br - coef_im * bi
    bbi_ref[...] = coef_re * bi + coef_im * br


def _s5_discretize(a_re, a_im, log_dt, b_re, b_im):
    g, p, m = b_re.shape
    rep = lambda a: jnp.repeat(a.astype(F32), m, axis=1)
    flat = lambda b: b.astype(F32).reshape(g, p * m)
    shp = jax.ShapeDtypeStruct((g, p * m), F32)
    abr, abi, bbr, bbi = pl.pallas_call(
        _disc_kernel, out_shape=(shp, shp, shp, shp), name="s5_disc",
    )(rep(a_re), rep(a_im), rep(log_dt), flat(b_re), flat(b_im))
    return abr[:, ::m], abi[:, ::m], bbr.reshape(g, p, m), bbi.reshape(g, p, m)


def _proj_kernel(x_ref, tab_ref, gmix_ref, wlat_ref, gq_ref, wuqn_ref, wuqr_ref, gkv_ref, wu_ref,
                 q_ref, ckv_ref, kr_ref, u_ref, *, q_lora, kv_lora, n_heads, scale):
    x = x_ref[...]
    xn = (_rms(x) * gmix_ref[...]).astype(BF16)
    z = jnp.dot(xn, wlat_ref[...], preferred_element_type=F32)
    cqn = (_rms(z[:, :q_lora]) * gq_ref[...]).astype(BF16)
    cos2, sin2, tab_k = tab_ref[:, :LANES], tab_ref[:, LANES:2 * LANES], tab_ref[:, 2 * LANES:]
    first_half = (lax.broadcasted_iota(jnp.int32, cos2.shape, 1) % 64) < 32
    for j in range(n_heads // 2):
        qn = (jnp.dot(cqn, wuqn_ref[:, j * 256:(j + 1) * 256], preferred_element_type=F32) * scale).astype(BF16)
        q_ref[2 * j, :, 0:LANES] = qn[:, :LANES]
        q_ref[2 * j + 1, :, 0:LANES] = qn[:, LANES:]
    for j in range(n_heads // 4):
        qr = jnp.dot(cqn, wuqr_ref[:, j * 256:(j + 1) * 256], preferred_element_type=F32) * scale
        for e in range(2):
            xr = qr[:, e * LANES:(e + 1) * LANES]
            swapped = jnp.where(first_half, pltpu.roll(xr, 96, 1), pltpu.roll(xr, 32, 1))
            rot = xr * cos2 + swapped * sin2
            ha = 4 * j + 2 * e
            q_ref[ha, :, LANES:2 * LANES] = rot.astype(BF16)
            q_ref[ha + 1, :, LANES:2 * LANES] = pltpu.roll(rot, 64, 1).astype(BF16)
    ckv_ref[...] = _rms(z[:, q_lora:q_lora + kv_lora]) * gkv_ref[...]
    kr = z[:, q_lora + kv_lora:] * tab_k
    kr = kr + pltpu.roll(kr, 64, 1)
    kr_ref[...] = kr[:, :64]
    u_ref[...] = jnp.dot(xn, wu_ref[...], preferred_element_type=F32)


def _proj(x2, tab, tab_blocks, gmix, wlat, gq, wuqn, wuqr, gkv, wu, *, tile, n_heads, q_lora, kv_lora, rope, scale):
    t_rows, d = x2.shape
    n_tiles = t_rows // tile
    kern = functools.partial(_proj_kernel, q_lora=q_lora, kv_lora=kv_lora, n_heads=n_heads, scale=scale)
    return pl.pallas_call(
        kern,
        grid=(n_tiles,),
        in_specs=[
            pl.BlockSpec((tile, d), lambda r: (r, 0)),
            pl.BlockSpec((tile, tab.shape[1]), lambda r: (r % tab_blocks, 0)),
            _const_spec(gmix.shape), _const_spec(wlat.shape), _const_spec(gq.shape),
            _const_spec(wuqn.shape), _const_spec(wuqr.shape), _const_spec(gkv.shape), _const_spec(wu.shape),
        ],
        out_specs=[
            pl.BlockSpec((n_heads, tile, 2 * LANES), lambda r: (0, r, 0)),
            pl.BlockSpec((tile, kv_lora), lambda r: (r, 0)),
            pl.BlockSpec((tile, rope), lambda r: (r, 0)),
            pl.BlockSpec((tile, wu.shape[1]), lambda r: (r, 0)),
        ],
        out_shape=[
            jax.ShapeDtypeStruct((n_heads, t_rows, 2 * LANES), BF16),
            jax.ShapeDtypeStruct((t_rows, kv_lora), F32),
            jax.ShapeDtypeStruct((t_rows, rope), F32),
            jax.ShapeDtypeStruct((t_rows, wu.shape[1]), F32),
        ],
        compiler_params=_cparams(1),
        name="proj",
    )(x2, tab, gmix, wlat, gq, wuqn, wuqr, gkv, wu)


def _kvup_kernel(ckv_ref, kr_ref, wuk_ref, wuv_ref, k_ref, v_ref, *, n_heads):
    c = ckv_ref[...].astype(BF16)
    kr = kr_ref[...]
    krz = jnp.concatenate([kr, jnp.zeros_like(kr)], axis=1).astype(BF16)
    ones = jnp.ones((c.shape[0], LANES), BF16)
    for hp in range(n_heads // 2):
        kn = jnp.dot(c, wuk_ref[:, hp * 256:(hp + 1) * 256], preferred_element_type=F32).astype(BF16)
        vv = jnp.dot(c, wuv_ref[:, hp * 256:(hp + 1) * 256], preferred_element_type=F32).astype(BF16)
        for s in range(2):
            h = 2 * hp + s
            k_ref[h, :, 0:LANES] = kn[:, s * LANES:(s + 1) * LANES]
            k_ref[h, :, LANES:2 * LANES] = krz
            v_ref[h, :, 0:LANES] = vv[:, s * LANES:(s + 1) * LANES]
            v_ref[h, :, LANES:2 * LANES] = ones


def _kvup(ckv2, kr2, wuk, wuv, *, tile, n_heads):
    t_rows, kv_lora = ckv2.shape
    rope = kr2.shape[1]
    kern = functools.partial(_kvup_kernel, n_heads=n_heads)
    return pl.pallas_call(
        kern,
        grid=(t_rows // tile,),
        in_specs=[
            pl.BlockSpec((tile, kv_lora), lambda r: (r, 0)),
            pl.BlockSpec((tile, rope), lambda r: (r, 0)),
            _const_spec(wuk.shape), _const_spec(wuv.shape),
        ],
        out_specs=[
            pl.BlockSpec((n_heads, tile, 2 * LANES), lambda r: (0, r, 0)),
            pl.BlockSpec((n_heads, tile, 2 * LANES), lambda r: (0, r, 0)),
        ],
        out_shape=[
            jax.ShapeDtypeStruct((n_heads, t_rows, 2 * LANES), BF16),
            jax.ShapeDtypeStruct((n_heads, t_rows, 2 * LANES), BF16),
        ],
        compiler_params=_cparams(1),
        name="kvup",
    )(ckv2, kr2, wuk, wuv)


def _attn_tile(q, k, v1, m_ref, acc_ref, mask):
    s = lax.dot_general(q, k, (((1,), (1,)), ((), ())), preferred_element_type=F32)
    if mask is not None:
        s = jnp.where(mask, s, -jnp.inf)
    cols = [s[:, c * LANES:(c + 1) * LANES] for c in range(s.shape[1] // LANES)]
    m_prev = m_ref[...]
    m_new = jnp.maximum(m_prev, jnp.max(functools.reduce(jnp.maximum, cols), axis=-1, keepdims=True))
    alpha = jnp.exp2(m_prev - m_new)
    p = jnp.concatenate([jnp.exp2(c - m_new).astype(BF16) for c in cols], axis=1)
    acc_ref[...] = (jnp.concatenate([alpha, alpha], axis=1) * acc_ref[...]
                    + jnp.dot(p, v1, preferred_element_type=F32))
    m_ref[...] = m_new


def _attn_init(m_ref, acc_ref):
    m_ref[...] = jnp.full(m_ref.shape, -jnp.inf, F32)
    acc_ref[...] = jnp.zeros(acc_ref.shape, F32)


def _attn_finish(acc_ref):
    acc = acc_ref[...]
    return acc[:, :LANES] / acc[:, LANES:]


def _chunk_mask(tq, tk, q_off, k_off):
    qc = (lax.broadcasted_iota(jnp.int32, (tq, tk), 0) + q_off) // CHUNK
    kc = (lax.broadcasted_iota(jnp.int32, (tq, tk), 1) + k_off) // CHUNK
    return kc <= qc


def _attn_causal_kernel(q_ref, k_ref, v_ref, o_ref, *stat_refs, tile, n_chains, n_qblocks):
    ms, accs = stat_refs[0::2], stat_refs[1::2]
    for c in range(n_chains):
        _attn_init(ms[c], accs[c])
    qs = [q_ref[c * tile:(c + 1) * tile, :] for c in range(n_chains)]

    if n_qblocks > 1:
        qi = pl.program_id(2)

        def full_tile(j, carry):
            off = pl.multiple_of(j * tile, tile)
            k = k_ref[pl.ds(off, tile), :]
            v1 = v_ref[pl.ds(off, tile), :]
            for c in range(n_chains):
                _attn_tile(qs[c], k, v1, ms[c], accs[c], None)
            return carry

        lax.fori_loop(0, qi * n_chains, full_tile, 0)
        base = qi * n_chains
    else:
        base = 0
    for d in range(n_chains):
        off = (base + d) * tile
        if n_qblocks > 1:
            off = pl.multiple_of(off, tile)
        k = k_ref[pl.ds(off, tile), :]
        v1 = v_ref[pl.ds(off, tile), :]
        for c in range(d, n_chains):
            _attn_tile(qs[c], k, v1, ms[c], accs[c], _chunk_mask(tile, tile, 0, 0) if c == d else None)
    for c in range(n_chains):
        o_ref[c * tile:(c + 1) * tile, :] = _attn_finish(accs[c]).astype(o_ref.dtype)


def _attention(q, k, v1, *, n_batch, seq, tile, n_chains):
    n_heads = q.shape[0]
    tq = tile * n_chains
    nq = seq // tq
    assert tile % CHUNK == 0 and seq % tq == 0
    stats = []
    for _ in range(n_chains):
        stats += [pltpu.VMEM((tile, LANES), F32), pltpu.VMEM((tile, 2 * LANES), F32)]
    return pl.pallas_call(
        functools.partial(_attn_causal_kernel, tile=tile, n_chains=n_chains, n_qblocks=nq),
        grid=(n_batch, n_heads, nq),
        in_specs=[
            pl.BlockSpec((None, tq, 2 * LANES), lambda b, h, i: (h, b * nq + i, 0)),
            pl.BlockSpec((None, seq, 2 * LANES), lambda b, h, i: (h, b, 0)),
            pl.BlockSpec((None, seq, 2 * LANES), lambda b, h, i: (h, b, 0)),
        ],
        out_specs=pl.BlockSpec((None, tq, LANES), lambda b, h, i: (h, b * nq + i, 0)),
        out_shape=jax.ShapeDtypeStruct((n_heads, n_batch * seq, LANES), BF16),
        scratch_shapes=stats,
        compiler_params=_cparams(3),
        name="attn_causal",
    )(q, k, v1)


def _attn_latent_kernel(q_ref, cc_ref, cn_ref, kc_ref, kn_ref, wuk_ref, wuv_ref, o_ref, *, q_pos0, n_heads):
    ls = q_ref.shape[1]
    nt = (((1,), (1,)), ((), ()))
    ckv = jnp.concatenate([cc_ref[...], cn_ref[...]], axis=0).astype(BF16)
    kr = jnp.concatenate([kc_ref[...], kn_ref[...]], axis=0)
    krz = jnp.concatenate([kr, jnp.zeros_like(kr)], axis=1).astype(BF16)
    q_lat = jnp.concatenate(
        [lax.dot_general(q_ref[h, :, :LANES], wuk_ref[:, h * LANES:(h + 1) * LANES], nt, preferred_element_type=F32)
         for h in range(n_heads)], axis=0).astype(BF16)
    q_rope = jnp.concatenate([q_ref[h, :, LANES:] for h in range(n_heads)], axis=0)
    s = (lax.dot_general(q_lat, ckv, nt, preferred_element_type=F32)
         + lax.dot_general(q_rope, krz, nt, preferred_element_type=F32))
    rows, lk = s.shape
    qc = (lax.rem(lax.broadcasted_iota(jnp.int32, (rows, lk), 0), ls) + q_pos0) // CHUNK
    kc = lax.broadcasted_iota(jnp.int32, (rows, lk), 1) // CHUNK
    s = jnp.where(kc <= qc, s, -jnp.inf)
    p = jnp.exp2(s - jnp.max(s, axis=-1, keepdims=True))
    l = jnp.sum(p, axis=-1, keepdims=True)
    o_lat = (jnp.dot(p.astype(BF16), ckv, preferred_element_type=F32) / l).astype(BF16)
    for h in range(n_heads):
        o_ref[h] = jnp.dot(o_lat[h * ls:(h + 1) * ls, :], wuv_ref[:, h * LANES:(h + 1) * LANES],
                           preferred_element_type=F32).astype(o_ref.dtype)


def _attention_latent(q, ckv_cache, ckv_new, kr_cache, kr_new, wuk, wuv, *, q_pos0):
    n_heads = q.shape[0]
    n_batch, past, kv_lora = ckv_cache.shape
    ls, rope = ckv_new.shape[1], kr_new.shape[2]
    kern = functools.partial(_attn_latent_kernel, q_pos0=q_pos0, n_heads=n_heads)
    return pl.pallas_call(
        kern,
        grid=(n_batch,),
        in_specs=[
            pl.BlockSpec((n_heads, ls, 2 * LANES), lambda b: (0, b, 0)),
            pl.BlockSpec((None, past, kv_lora), lambda b: (b, 0, 0)),
            pl.BlockSpec((None, ls, kv_lora), lambda b: (b, 0, 0)),
            pl.BlockSpec((None, past, rope), lambda b: (b, 0, 0)),
            pl.BlockSpec((None, ls, rope), lambda b: (b, 0, 0)),
            _const_spec(wuk.shape), _const_spec(wuv.shape),
        ],
        out_specs=pl.BlockSpec((n_heads, ls, LANES), lambda b: (0, b, 0)),
        out_shape=jax.ShapeDtypeStruct((n_heads, n_batch * ls, LANES), BF16),
        compiler_params=_cparams(1),
        name="attn_latent",
    )(q, ckv_cache, ckv_new, kr_cache, kr_new, wuk, wuv)


def _s5_kernel(u_ref, h0_ref, a_ref, wb_ref, wc_ref, d_ref, g_ref, hout_ref, *scratch, tl, n_batch, ns):
    i = pl.program_id(1)
    half = a_ref.shape[-1] // 2
    rows = tl * n_batch
    x_refs, hs_ref = scratch[:ns], scratch[ns]

    @pl.when(i == 0)
    def _():
        hs_ref[...] = h0_ref[...]

    ut = jnp.swapaxes(u_ref[...], 0, 1).reshape(rows, ns * LANES)
    for s in range(ns):
        x_ref = x_refs[s]
        u = ut[:, s * LANES:(s + 1) * LANES]
        ub = u.astype(BF16)
        x_ref[:, :half] = jnp.dot(ub, wb_ref[s, :, :half], preferred_element_type=F32)
        x_ref[:, half:] = jnp.dot(ub, wb_ref[s, :, half:], preferred_element_type=F32)
        ar = jnp.broadcast_to(a_ref[s, :, :half], (n_batch, half))
        ai = jnp.broadcast_to(a_ref[s, :, half:], (n_batch, half))
        hr, hi = hs_ref[s, :, :half], hs_ref[s, :, half:]
        for t in range(tl):
            r0 = t * n_batch
            nr = ar * hr - ai * hi + x_ref[r0:r0 + n_batch, :half]
            ni = ar * hi + ai * hr + x_ref[r0:r0 + n_batch, half:]
            x_ref[r0:r0 + n_batch, :half] = nr
            x_ref[r0:r0 + n_batch, half:] = ni
            hr, hi = nr, ni
        hs_ref[s, :, :half] = hr
        hs_ref[s, :, half:] = hi

        y = (jnp.dot(x_ref[:, :half].astype(BF16), wc_ref[s, :half, :], preferred_element_type=F32)
             + jnp.dot(x_ref[:, half:].astype(BF16), wc_ref[s, half:, :], preferred_element_type=F32))
        g = jax.nn.gelu(y + d_ref[s] * u)
        g_ref[:, :, s * LANES:(s + 1) * LANES] = jnp.swapaxes(
            g.reshape(tl, n_batch, LANES), 0, 1).astype(g_ref.dtype)

    @pl.when(i == pl.num_programs(1) - 1)
    def _():
        hout_ref[...] = hs_ref[...]


def _s5(u3, h0, a_l, wb, wc, d_l, *, tl, ns):
    n_batch, seq, width = u3.shape
    n_slabs = width // LANES
    st = a_l.shape[-1]
    kern = functools.partial(_s5_kernel, tl=tl, n_batch=n_batch, ns=ns)
    return pl.pallas_call(
        kern,
        grid=(n_slabs // ns, seq // tl),
        in_specs=[
            pl.BlockSpec((n_batch, tl, ns * LANES), lambda j, i: (0, i, j)),
            pl.BlockSpec((ns, n_batch, st), lambda j, i: (j, 0, 0)),
            pl.BlockSpec((ns, 1, st), lambda j, i: (j, 0, 0)),
            pl.BlockSpec((ns, LANES, st), lambda j, i: (j, 0, 0)),
            pl.BlockSpec((ns, st, LANES), lambda j, i: (j, 0, 0)),
            pl.BlockSpec((ns, 1, LANES), lambda j, i: (j, 0, 0)),
        ],
        out_specs=[
            pl.BlockSpec((n_batch, tl, ns * LANES), lambda j, i: (0, i, j)),
            pl.BlockSpec((ns, n_batch, st), lambda j, i: (j, 0, 0)),
        ],
        out_shape=[
            jax.ShapeDtypeStruct((n_batch, seq, width), BF16),
            jax.ShapeDtypeStruct((n_slabs, n_batch, st), F32),
        ],
        scratch_shapes=[pltpu.VMEM((tl * n_batch, st), F32) for _ in range(ns)] + [pltpu.VMEM((ns, n_batch, st), F32)],
        compiler_params=_cparams(2),
        name="s5",
    )(u3, h0, a_l, wb, wc, d_l)


def _merge_kernel(x_ref, g_ref, attn_ref, gmix_ref, wga_ref, wgb_ref, wa_ref, wb_ref, wo_ref,
                  h_ref, xn_ref, acc_ref):
    j = pl.program_id(1)

    @pl.when(j == 0)
    def _():
        xn_ref[...] = (_rms(x_ref[...]) * gmix_ref[...]).astype(BF16)
        acc_ref[...] = jnp.zeros(acc_ref.shape, F32)

    xn = xn_ref[...]
    g = g_ref[...]
    ga = jax.nn.sigmoid(jnp.dot(xn, wga_ref[...], preferred_element_type=F32))
    gb = jax.nn.sigmoid(jnp.dot(xn, wgb_ref[...], preferred_element_type=F32))
    s5o = (jnp.dot(g, wa_ref[...], preferred_element_type=F32)
           * jax.nn.sigmoid(jnp.dot(g, wb_ref[...], preferred_element_type=F32)))
    attn = jnp.concatenate([attn_ref[h].astype(F32) for h in range(attn_ref.shape[0])], axis=1)
    mix = (ga * attn + gb * s5o).astype(BF16)
    acc_ref[...] += jnp.dot(mix, wo_ref[...], preferred_element_type=F32)

    @pl.when(j == pl.num_programs(1) - 1)
    def _():
        h_ref[...] = x_ref[...] + acc_ref[...]


def _merge(x2, g2, attn, gmix, wga, wgb, wa, wb, wo, *, tile, cb):
    t_rows, d = x2.shape
    hpb = cb // LANES
    return pl.pallas_call(
        _merge_kernel,
        grid=(t_rows // tile, d // cb),
        in_specs=[
            pl.BlockSpec((tile, d), lambda r, j: (r, 0)),
            pl.BlockSpec((tile, d), lambda r, j: (r, 0)),
            pl.BlockSpec((hpb, tile, LANES), lambda r, j: (j, r, 0)),
            _const_spec(gmix.shape),
            pl.BlockSpec((d, cb), lambda r, j: (0, j)),
            pl.BlockSpec((d, cb), lambda r, j: (0, j)),
            pl.BlockSpec((d, cb), lambda r, j: (0, j)),
            pl.BlockSpec((d, cb), lambda r, j: (0, j)),
            pl.BlockSpec((cb, d), lambda r, j: (j, 0)),
        ],
        out_specs=pl.BlockSpec((tile, d), lambda r, j: (r, 0)),
        out_shape=jax.ShapeDtypeStruct((t_rows, d), F32),
        scratch_shapes=[pltpu.VMEM((tile, d), BF16), pltpu.VMEM((tile, d), F32)],
        compiler_params=_cparams(2),
        name="merge",
    )(x2, g2, attn, gmix, wga, wgb, wa, wb, wo)


def _mlp_kernel(h_ref, gmlp_ref, wup_ref, wdn_ref, gfin_ref, y_ref, hn_ref, acc_ref, *, final_norm):
    j = pl.program_id(1)

    @pl.when(j == 0)
    def _():
        hn_ref[...] = (_rms(h_ref[...]) * gmlp_ref[...]).astype(BF16)
        acc_ref[...] = jnp.zeros(acc_ref.shape, F32)

    a = jnp.maximum(jnp.dot(hn_ref[...], wup_ref[...], preferred_element_type=F32), 0.0)
    acc_ref[...] += jnp.dot((a * a).astype(BF16), wdn_ref[...], preferred_element_type=F32)

    @pl.when(j == pl.num_programs(1) - 1)
    def _():
        y = h_ref[...] + acc_ref[...]
        y_ref[...] = _rms(y) * gfin_ref[...] if final_norm else y


def _mlp(h2, gmlp, wup, wdn, gfin, *, tile, fb, final_norm):
    t_rows, d = h2.shape
    d_ff = wup.shape[1]
    return pl.pallas_call(
        functools.partial(_mlp_kernel, final_norm=final_norm),
        grid=(t_rows // tile, d_ff // fb),
        in_specs=[
            pl.BlockSpec((tile, d), lambda r, j: (r, 0)),
            _const_spec(gmlp.shape),
            pl.BlockSpec((d, fb), lambda r, j: (0, j)),
            pl.BlockSpec((fb, d), lambda r, j: (j, 0)),
            _const_spec(gfin.shape),
        ],
        out_specs=pl.BlockSpec((tile, d), lambda r, j: (r, 0)),
        out_shape=jax.ShapeDtypeStruct((t_rows, d), F32),
        scratch_shapes=[pltpu.VMEM((tile, d), BF16), pltpu.VMEM((tile, d), F32)],
        compiler_params=_cparams(2),
        name="mlp",
    )(h2, gmlp, wup, wdn, gfin)


def _rope_table(pos, rope):
    half = rope // 2
    inv = ROPE_THETA ** (-jnp.arange(half, dtype=F32) / half)
    ang = pos.astype(F32)[:, None] * inv[None, :]
    cos, sin = jnp.cos(ang), jnp.sin(ang)
    return jnp.concatenate([cos, cos, cos, cos, -sin, sin, -sin, sin, cos, cos, -sin, sin], axis=1)


def _swap_halves(w):
    half = w.shape[-1] // 2
    return jnp.concatenate([w[..., half:], w[..., :half]], axis=-1)


def _tiles(n_batch, seq, causal):
    rows = n_batch * seq
    t = dict(
        proj=_pick_tile(seq, PROJ_ROWS, 16) if seq >= PROJ_ROWS else rows,
        merge=_pick_tile(rows, MERGE_ROWS, 16),
        mlp=_pick_tile(rows, MLP_ROWS, 16),
        s5_tl=_pick_tile(seq, S5_TL, SUBLANES),
        attn=None, chains=1,
    )
    if causal:
        t["attn"] = _pick_tile(seq, ATTN_TILE, CHUNK)
        t["chains"] = max(c for c in range(1, ATTN_CHAINS + 1) if (seq // t["attn"]) % c == 0)
    return t


def _group(x, pos0, w, s5w, h0, cache):
    n_batch, seq, d = x.shape
    t_rows = n_batch * seq
    n_heads, kv_lora, rope = w["n_heads"], w["kv_lora"], w["rope"]
    x2 = x.reshape(t_rows, d)
    tiles = _tiles(n_batch, seq, causal=cache is None)
    row_tile = tiles["proj"]

    pos = pos0 + jnp.arange(seq)
    tab = _rope_table(pos, rope)
    if row_tile > seq:
        assert row_tile % seq == 0
        tab = jnp.tile(tab, (row_tile // seq, 1))
        tab_blocks = 1
    else:
        assert seq % row_tile == 0
        tab_blocks = seq // row_tile

    q, ckv, kr, u = _proj(x2, tab, tab_blocks, w["gmix"], w["wlat"], w["gq"], w["wuqn"], w["wuqr"], w["gkv"], w["wu"],
                          tile=row_tile, n_heads=n_heads, q_lora=w["q_lora"], kv_lora=kv_lora, rope=rope,
                          scale=w["scale"])
    ckv3 = ckv.reshape(n_batch, seq, kv_lora)
    kr3 = kr.reshape(n_batch, seq, rope)
    if cache is not None:
        attn = _attention_latent(q, cache[0].astype(F32), ckv3, cache[1].astype(F32), kr3, w["wuk"], w["wuv"],
                                 q_pos0=pos0)
    else:
        k, v1 = _kvup(ckv, kr, w["wuk"], w["wuv"], tile=_pick_tile(t_rows, KV_ROWS, 16), n_heads=n_heads)
        attn = _attention(q, k, v1, n_batch=n_batch, seq=seq, tile=tiles["attn"], n_chains=tiles["chains"])

    n_slabs = s5w["a"].shape[0]
    g, h_fin = _s5(u.reshape(n_batch, seq, -1), h0, s5w["a"], s5w["wb"], s5w["wc"], s5w["d"], tl=tiles["s5_tl"],
                   ns=math.gcd(n_slabs, S5_SLABS))

    h = _merge(x2, g.reshape(t_rows, -1), attn, w["gmix"], w["wga"], w["wgb"], w["wglu_a"], w["wglu_b"], w["wo"],
               tile=tiles["merge"], cb=4 * LANES)
    return h, ckv3, kr3, h_fin, tiles["mlp"]


def _state_to_slabs(s_re, s_im, n_slabs):
    n_batch, g, p = s_re.shape
    f = lambda s: s.astype(F32).reshape(n_batch, n_slabs, (g // n_slabs) * p).swapaxes(0, 1)
    return jnp.concatenate([f(s_re), f(s_im)], axis=-1)


def _slabs_to_state(h, g, p):
    n_slabs, n_batch, st = h.shape
    half = st // 2
    f = lambda s: s.swapaxes(0, 1).reshape(n_batch, g, p)
    return f(h[..., :half]), f(h[..., half:])


def kernel(x_prompt, x_sample, cache_ckv, cache_krope, state_s5_re, state_s5_im, norm_mix, w_in, norm_q, w_uq, norm_kv, w_uk, w_uv, s5_a_re, s5_a_im, s5_log_dt, s5_b_re, s5_b_im, s5_c_re, s5_c_im, s5_d, w_glu_a, w_glu_b, w_o, norm_mlp, w_up, w_down, norm_final):
    depth = w_in.shape[0]
    d = x_prompt.shape[-1]
    q_lora, kv_lora = norm_q.shape[1], norm_kv.shape[1]
    n_heads, nope = w_uk.shape[2], w_uk.shape[3]
    v_dim = w_uv.shape[3]
    rope = cache_krope.shape[-1]
    n_groups, n_state, grp = s5_b_re.shape[1:]
    width = n_groups * grp
    past_len = cache_ckv.shape[2]
    sdt = state_s5_re.dtype
    assert nope == LANES and v_dim == LANES and 2 * rope == LANES and d % (4 * LANES) == 0
    assert width % LANES == 0 and LANES % grp == 0 and x_prompt.shape[0] == SUBLANES == x_sample.shape[0]
    n_slabs = width // LANES
    gps = n_groups // n_slabs
    half = gps * n_state
    scale = math.log2(math.e) / math.sqrt(nope + rope)
    c1 = q_lora + kv_lora
    c2 = c1 + rope
    c3 = c2 + width
    c4 = c3 + d

    hp, hs = x_prompt, x_sample
    outs_p, outs_s = [], []
    for l in range(depth):
        wi = w_in[l]
        w_kr = wi[:, c1:c2]
        wuq3 = w_uq[l].reshape(q_lora, n_heads, nope + rope)
        w = dict(
            n_heads=n_heads, q_lora=q_lora, kv_lora=kv_lora, rope=rope, scale=scale,
            gmix=norm_mix[l][None].astype(F32), gq=norm_q[l][None].astype(F32), gkv=norm_kv[l][None].astype(F32),
            wlat=jnp.concatenate([wi[:, :c1], w_kr, _swap_halves(w_kr)], axis=1).astype(BF16),
            wuqn=wuq3[..., :nope].reshape(q_lora, n_heads * nope).astype(BF16),
            wuqr=wuq3[..., nope:].reshape(q_lora, n_heads * rope).astype(BF16),
            wu=wi[:, c2:c3].astype(BF16), wga=wi[:, c3:c4].astype(BF16), wgb=wi[:, c4:].astype(BF16),
            wuk=w_uk[l].reshape(kv_lora, n_heads * nope).astype(BF16),
            wuv=w_uv[l].reshape(kv_lora, n_heads * v_dim).astype(BF16),
            wglu_a=w_glu_a[l].astype(BF16), wglu_b=w_glu_b[l].astype(BF16), wo=w_o[l].astype(BF16),
        )
        abr, abi, bbr, bbi = _s5_discretize(s5_a_re[l], s5_a_im[l], s5_log_dt[l], s5_b_re[l], s5_b_im[l])
        on_diag = jnp.eye(gps, dtype=jnp.bool_)[None, :, None, :, None]
        bd_in = lambda b: jnp.where(
            on_diag, b.reshape(n_slabs, gps, n_state, grp).swapaxes(2, 3)[:, :, :, None, :], 0.0).reshape(
                n_slabs, LANES, half)
        bd_out = lambda c: jnp.where(
            on_diag, c.astype(F32).reshape(n_slabs, gps, grp, n_state).swapaxes(2, 3)[:, :, :, None, :], 0.0).reshape(
                n_slabs, half, LANES)
        s5w = dict(
            a=jnp.concatenate([abr.reshape(n_slabs, 1, half), abi.reshape(n_slabs, 1, half)], axis=-1),
            wb=jnp.concatenate([bd_in(bbr), bd_in(bbi)], axis=-1).astype(BF16),
            wc=jnp.concatenate([bd_out(s5_c_re[l]), -bd_out(s5_c_im[l])], axis=1).astype(BF16),
            d=s5_d[l].astype(F32).reshape(n_slabs, 1, LANES),
        )
        gmlp, gfin = norm_mlp[l][None].astype(F32), norm_final[None].astype(F32)
        wup, wdn = w_up[l].astype(BF16), w_down[l].astype(BF16)

        bp, lp, _ = hp.shape
        h0 = jnp.zeros((n_slabs, bp, 2 * half), F32)
        last = l == depth - 1
        h, ckv, kr, hfin, mlp_tile = _group(hp, 0, w, s5w, h0, None)
        outs_p.append((ckv, kr) + _slabs_to_state(hfin, n_groups, n_state))
        hp = _mlp(h, gmlp, wup, wdn, gfin, tile=mlp_tile, fb=FF_BLOCK, final_norm=last).reshape(bp, lp, d)

        bs, ls, _ = hs.shape
        h0 = _state_to_slabs(state_s5_re[l], state_s5_im[l], n_slabs)
        h, ckv, kr, hfin, mlp_tile = _group(hs, past_len, w, s5w, h0, (cache_ckv[l], cache_krope[l]))
        outs_s.append((ckv, kr) + _slabs_to_state(hfin, n_groups, n_state))
        hs = _mlp(h, gmlp, wup, wdn, gfin, tile=mlp_tile, fb=FF_BLOCK, final_norm=last).reshape(bs, ls, d)

    def stack(outs, i, dt):
        if len(outs) == 1:
            return outs[0][i].astype(dt)[None]
        return jnp.stack([o[i].astype(dt) for o in outs], axis=0)
    return (hp, hs,
            stack(outs_p, 0, F32), stack(outs_p, 1, F32), stack(outs_p, 2, sdt), stack(outs_p, 3, sdt),
            stack(outs_s, 0, F32), stack(outs_s, 1, F32), stack(outs_s, 2, sdt), stack(outs_s, 3, sdt))
```

```python
import functools
import math

import jax
import jax.numpy as jnp
from jax import lax
from jax.experimental import pallas as pl
from jax.experimental.pallas import tpu as pltpu

F32 = jnp.float32
BF16 = jnp.bfloat16

EPS = 1e-6
CHUNK = 64
ROPE_THETA = 10000.0
LANES = 128
SUBLANES = 8
VMEM_LIMIT = 56 * 1024 * 1024

PROJ_ROWS = 256
MERGE_ROWS = 512
MLP_ROWS = 512
FF_BLOCK = 1024
ATTN_TILE = 512
ATTN_CHAINS = 8
S5_TL = 128
S5_SLABS = 4


def _cparams(n_axes):
    return pltpu.CompilerParams(dimension_semantics=("arbitrary",) * n_axes,
                                vmem_limit_bytes=VMEM_LIMIT)


def _const_spec(shape):
    nd = len(shape)
    return pl.BlockSpec(shape, lambda *_: (0,) * nd, pipeline_mode=pl.Buffered(1))


def _rms(x):
    return x * lax.rsqrt(jnp.mean(x * x, axis=-1, keepdims=True) + EPS)


def _pick_tile(n, cap, mult):
    best = None
    for t in range(mult, min(n, cap) + 1, mult):
        if n % t == 0:
            best = t
    assert best is not None, (n, cap, mult)
    return best


def _disc_kernel(ar_ref, ai_ref, ldt_ref, br_ref, bi_ref, abr_ref, abi_ref, bbr_ref, bbi_ref):
    ar = ar_ref[...]
    ai = ai_ref[...]
    dt = jnp.exp(ldt_ref[...])
    mag = jnp.exp(ar * dt)
    abar_re = mag * jnp.cos(ai * dt)
    abar_im = mag * jnp.sin(ai * dt)
    den = ar * ar + ai * ai
    nr = abar_re - 1.0
    coef_re = (nr * ar + abar_im * ai) / den
    coef_im = (abar_im * ar - nr * ai) / den
    br = br_ref[...]
    bi = bi_ref[...]
    abr_ref[...] = abar_re
    abi_ref[...] = abar_im
    bbr_ref[...] = coef_re * br - coef_im * bi
    bbi_ref[...] = coef_re * bi + coef_im * br


def _s5_discretize(a_re, a_im, log_dt, b_re, b_im):
    g, p, m = b_re.shape
    rep = lambda a: jnp.repeat(a.astype(F32), m, axis=1)
    flat = lambda b: b.astype(F32).reshape(g, p * m)
    shp = jax.ShapeDtypeStruct((g, p * m), F32)
    abr, abi, bbr, bbi = pl.pallas_call(
        _disc_kernel, out_shape=(shp, shp, shp, shp), name="s5_disc",
    )(rep(a_re), rep(a_im), rep(log_dt), flat(b_re), flat(b_im))
    return abr[:, ::m], abi[:, ::m], bbr.reshape(g, p, m), bbi.reshape(g, p, m)


def _proj_kernel(x_ref, tab_ref, gmix_ref, wlat_ref, gq_ref, wuqn_ref, wuqr_ref, gkv_ref, wu_ref, *rest,
                 q_lora, kv_lora, n_heads, scale, with_kv):
    if with_kv:
        wuk_ref, wuv_ref, q_ref, ckv_ref, kr_ref, u_ref, k_ref, v_ref = rest
    else:
        q_ref, ckv_ref, kr_ref, u_ref = rest
    x = x_ref[...]
    xn = (_rms(x) * gmix_ref[...]).astype(BF16)
    z = jnp.dot(xn, wlat_ref[...], preferred_element_type=F32)
    cqn = (_rms(z[:, :q_lora]) * gq_ref[...]).astype(BF16)
    cos2, sin2, tab_k = tab_ref[:, :LANES], tab_ref[:, LANES:2 * LANES], tab_ref[:, 2 * LANES:]
    first_half = (lax.broadcasted_iota(jnp.int32, cos2.shape, 1) % 64) < 32
    for j in range(n_heads // 2):
        qn = (jnp.dot(cqn, wuqn_ref[:, j * 256:(j + 1) * 256], preferred_element_type=F32) * scale).astype(BF16)
        q_ref[2 * j, :, 0:LANES] = qn[:, :LANES]
        q_ref[2 * j + 1, :, 0:LANES] = qn[:, LANES:]
    for j in range(n_heads // 4):
        qr = jnp.dot(cqn, wuqr_ref[:, j * 256:(j + 1) * 256], preferred_element_type=F32) * scale
        for e in range(2):
            xr = qr[:, e * LANES:(e + 1) * LANES]
            swapped = jnp.where(first_half, pltpu.roll(xr, 96, 1), pltpu.roll(xr, 32, 1))
            rot = xr * cos2 + swapped * sin2
            ha = 4 * j + 2 * e
            q_ref[ha, :, LANES:2 * LANES] = rot.astype(BF16)
            q_ref[ha + 1, :, LANES:2 * LANES] = pltpu.roll(rot, 64, 1).astype(BF16)
    ckv = _rms(z[:, q_lora:q_lora + kv_lora]) * gkv_ref[...]
    ckv_ref[...] = ckv
    kr = z[:, q_lora + kv_lora:] * tab_k
    kr = kr + pltpu.roll(kr, 64, 1)
    kr_ref[...] = kr[:, :64]

    u = jnp.dot(xn, wu_ref[...], preferred_element_type=F32)
    width = wu_ref.shape[1]
    n_seqs = u_ref.shape[1] // width
    seq = u_ref.shape[0]
    for b in range(n_seqs):
        u_ref[:, b * width:(b + 1) * width] = u[b * seq:(b + 1) * seq, :]

    if with_kv:
        c = ckv.astype(BF16)
        krz = jnp.where(lax.broadcasted_iota(jnp.int32, kr.shape, 1) < 64, kr, 0.0).astype(BF16)
        ones = jnp.ones((c.shape[0], LANES), BF16)
        for hp in range(n_heads // 2):
            kn = jnp.dot(c, wuk_ref[:, hp * 256:(hp + 1) * 256], preferred_element_type=F32).astype(BF16)
            vv = jnp.dot(c, wuv_ref[:, hp * 256:(hp + 1) * 256], preferred_element_type=F32).astype(BF16)
            for s in range(2):
                h = 2 * hp + s
                k_ref[h, :, 0:LANES] = kn[:, s * LANES:(s + 1) * LANES]
                k_ref[h, :, LANES:2 * LANES] = krz
                v_ref[h, :, 0:LANES] = vv[:, s * LANES:(s + 1) * LANES]
                v_ref[h, :, LANES:2 * LANES] = ones


def _proj(x2, tab, tab_blocks, gmix, wlat, gq, wuqn, wuqr, gkv, wu, wkv, *, n_batch, tile, n_heads, q_lora, kv_lora,
          rope, scale):
    t_rows, d = x2.shape
    seq = t_rows // n_batch
    width = wu.shape[1]
    n_tiles = t_rows // tile
    with_kv = wkv is not None
    kern = functools.partial(_proj_kernel, q_lora=q_lora, kv_lora=kv_lora, n_heads=n_heads, scale=scale,
                             with_kv=with_kv)
    if tile <= seq:
        tps = seq // tile
        u_spec = pl.BlockSpec((tile, width), lambda r: (r % tps, r // tps))
    else:
        assert tile % seq == 0
        u_spec = pl.BlockSpec((seq, (tile // seq) * width), lambda r: (0, r))
    head_spec = pl.BlockSpec((n_heads, tile, 2 * LANES), lambda r: (0, r, 0))
    head_shape = jax.ShapeDtypeStruct((n_heads, t_rows, 2 * LANES), BF16)
    weights = (gmix, wlat, gq, wuqn, wuqr, gkv, wu) + (tuple(wkv) if with_kv else ())
    return pl.pallas_call(
        kern,
        grid=(n_tiles,),
        in_specs=[
            pl.BlockSpec((tile, d), lambda r: (r, 0)),
            pl.BlockSpec((tile, tab.shape[1]), lambda r: (r % tab_blocks, 0)),
        ] + [_const_spec(a.shape) for a in weights],
        out_specs=[
            head_spec,
            pl.BlockSpec((tile, kv_lora), lambda r: (r, 0)),
            pl.BlockSpec((tile, rope), lambda r: (r, 0)),
            u_spec,
        ] + ([head_spec, head_spec] if with_kv else []),
        out_shape=[
            head_shape,
            jax.ShapeDtypeStruct((t_rows, kv_lora), F32),
            jax.ShapeDtypeStruct((t_rows, rope), F32),
            jax.ShapeDtypeStruct((seq, n_batch * width), F32),
        ] + ([head_shape, head_shape] if with_kv else []),
        compiler_params=_cparams(1),
        name="proj",
    )(x2, tab, *weights)


def _attn_tile(q, k, v1, m_ref, acc_ref, mask):
    s = lax.dot_general(q, k, (((1,), (1,)), ((), ())), preferred_element_type=F32)
    if mask is not None:
        s = jnp.where(mask, s, -jnp.inf)
    cols = [s[:, c * LANES:(c + 1) * LANES] for c in range(s.shape[1] // LANES)]
    m_prev = m_ref[...]
    m_new = jnp.maximum(m_prev, jnp.max(functools.reduce(jnp.maximum, cols), axis=-1, keepdims=True))
    alpha = jnp.exp2(m_prev - m_new)
    p = jnp.concatenate([jnp.exp2(c - m_new).astype(BF16) for c in cols], axis=1)
    acc_ref[...] = (jnp.concatenate([alpha, alpha], axis=1) * acc_ref[...]
                    + jnp.dot(p, v1, preferred_element_type=F32))
    m_ref[...] = m_new


def _attn_init(m_ref, acc_ref):
    m_ref[...] = jnp.full(m_ref.shape, -jnp.inf, F32)
    acc_ref[...] = jnp.zeros(acc_ref.shape, F32)


def _attn_finish(acc_ref):
    acc = acc_ref[...]
    return acc[:, :LANES] / acc[:, LANES:]


def _chunk_mask(tq, tk, q_off, k_off):
    qc = (lax.broadcasted_iota(jnp.int32, (tq, tk), 0) + q_off) // CHUNK
    kc = (lax.broadcasted_iota(jnp.int32, (tq, tk), 1) + k_off) // CHUNK
    return kc <= qc


def _attn_causal_kernel(q_ref, k_ref, v_ref, o_ref, *stat_refs, tile, n_chains, n_qblocks):
    ms, accs = stat_refs[0::2], stat_refs[1::2]
    for c in range(n_chains):
        _attn_init(ms[c], accs[c])
    qs = [q_ref[c * tile:(c + 1) * tile, :] for c in range(n_chains)]

    if n_qblocks > 1:
        qi = pl.program_id(2)

        def full_tile(j, carry):
            off = pl.multiple_of(j * tile, tile)
            k = k_ref[pl.ds(off, tile), :]
            v1 = v_ref[pl.ds(off, tile), :]
            for c in range(n_chains):
                _attn_tile(qs[c], k, v1, ms[c], accs[c], None)
            return carry

        lax.fori_loop(0, qi * n_chains, full_tile, 0)
        base = qi * n_chains
    else:
        base = 0
    for d in range(n_chains):
        off = (base + d) * tile
        if n_qblocks > 1:
            off = pl.multiple_of(off, tile)
        k = k_ref[pl.ds(off, tile), :]
        v1 = v_ref[pl.ds(off, tile), :]
        for c in range(d, n_chains):
            _attn_tile(qs[c], k, v1, ms[c], accs[c], _chunk_mask(tile, tile, 0, 0) if c == d else None)
    for c in range(n_chains):
        o_ref[c * tile:(c + 1) * tile, :] = _attn_finish(accs[c]).astype(o_ref.dtype)


def _attention(q, k, v1, *, n_batch, seq, tile, n_chains):
    n_heads = q.shape[0]
    tq = tile * n_chains
    nq = seq // tq
    assert tile % CHUNK == 0 and seq % tq == 0
    stats = []
    for _ in range(n_chains):
        stats += [pltpu.VMEM((tile, LANES), F32), pltpu.VMEM((tile, 2 * LANES), F32)]
    return pl.pallas_call(
        functools.partial(_attn_causal_kernel, tile=tile, n_chains=n_chains, n_qblocks=nq),
        grid=(n_batch, n_heads, nq),
        in_specs=[
            pl.BlockSpec((None, tq, 2 * LANES), lambda b, h, i: (h, b * nq + i, 0)),
            pl.BlockSpec((None, seq, 2 * LANES), lambda b, h, i: (h, b, 0)),
            pl.BlockSpec((None, seq, 2 * LANES), lambda b, h, i: (h, b, 0)),
        ],
        out_specs=pl.BlockSpec((None, tq, LANES), lambda b, h, i: (h, b * nq + i, 0)),
        out_shape=jax.ShapeDtypeStruct((n_heads, n_batch * seq, LANES), BF16),
        scratch_shapes=stats,
        compiler_params=_cparams(3),
        name="attn_causal",
    )(q, k, v1)


def _attn_latent_kernel(q_ref, cc_ref, cn_ref, kc_ref, kn_ref, wuk_ref, wuv_ref, o_ref, *, q_pos0, n_heads):
    ls = q_ref.shape[1]
    nt = (((1,), (1,)), ((), ()))
    ckv = jnp.concatenate([cc_ref[...], cn_ref[...]], axis=0).astype(BF16)
    kr = jnp.concatenate([kc_ref[...], kn_ref[...]], axis=0)
    krz = jnp.concatenate([kr, jnp.zeros_like(kr)], axis=1).astype(BF16)
    q_lat = jnp.concatenate(
        [lax.dot_general(q_ref[h, :, :LANES], wuk_ref[:, h * LANES:(h + 1) * LANES], nt, preferred_element_type=F32)
         for h in range(n_heads)], axis=0).astype(BF16)
    q_rope = jnp.concatenate([q_ref[h, :, LANES:] for h in range(n_heads)], axis=0)
    s = (lax.dot_general(q_lat, ckv, nt, preferred_element_type=F32)
         + lax.dot_general(q_rope, krz, nt, preferred_element_type=F32))
    rows, lk = s.shape
    qc = (lax.rem(lax.broadcasted_iota(jnp.int32, (rows, lk), 0), ls) + q_pos0) // CHUNK
    kc = lax.broadcasted_iota(jnp.int32, (rows, lk), 1) // CHUNK
    s = jnp.where(kc <= qc, s, -jnp.inf)
    p = jnp.exp2(s - jnp.max(s, axis=-1, keepdims=True))
    l = jnp.sum(p, axis=-1, keepdims=True)
    o_lat = (jnp.dot(p.astype(BF16), ckv, preferred_element_type=F32) / l).astype(BF16)
    for h in range(n_heads):
        o_ref[h] = jnp.dot(o_lat[h * ls:(h + 1) * ls, :], wuv_ref[:, h * LANES:(h + 1) * LANES],
                           preferred_element_type=F32).astype(o_ref.dtype)


def _attention_latent(q, ckv_cache, ckv_new, kr_cache, kr_new, wuk, wuv, *, q_pos0):
    n_heads = q.shape[0]
    n_batch, past, kv_lora = ckv_cache.shape
    ls, rope = ckv_new.shape[1], kr_new.shape[2]
    kern = functools.partial(_attn_latent_kernel, q_pos0=q_pos0, n_heads=n_heads)
    return pl.pallas_call(
        kern,
        grid=(n_batch,),
        in_specs=[
            pl.BlockSpec((n_heads, ls, 2 * LANES), lambda b: (0, b, 0)),
            pl.BlockSpec((None, past, kv_lora), lambda b: (b, 0, 0)),
            pl.BlockSpec((None, ls, kv_lora), lambda b: (b, 0, 0)),
            pl.BlockSpec((None, past, rope), lambda b: (b, 0, 0)),
            pl.BlockSpec((None, ls, rope), lambda b: (b, 0, 0)),
            _const_spec(wuk.shape), _const_spec(wuv.shape),
        ],
        out_specs=pl.BlockSpec((n_heads, ls, LANES), lambda b: (0, b, 0)),
        out_shape=jax.ShapeDtypeStruct((n_heads, n_batch * ls, LANES), BF16),
        compiler_params=_cparams(1),
        name="attn_latent",
    )(q, ckv_cache, ckv_new, kr_cache, kr_new, wuk, wuv)


def _s5_kernel(u_ref, h0_ref, a_ref, wb_ref, wc_ref, d_ref, g_ref, hout_ref, *scratch, tl, n_batch, ns):
    i = pl.program_id(1)
    half = a_ref.shape[-1] // 2
    rows = tl * n_batch
    x_refs, hs_ref = scratch[:ns], scratch[ns]

    @pl.when(i == 0)
    def _():
        hs_ref[...] = h0_ref[...]

    ut = u_ref[...].reshape(rows, ns * LANES)
    for s in range(ns):
        x_ref = x_refs[s]
        u = ut[:, s * LANES:(s + 1) * LANES]
        ub = u.astype(BF16)
        x_ref[:, :half] = jnp.dot(ub, wb_ref[s, :, :half], preferred_element_type=F32)
        x_ref[:, half:] = jnp.dot(ub, wb_ref[s, :, half:], preferred_element_type=F32)
        ar = jnp.broadcast_to(a_ref[s, :, :half], (n_batch, half))
        ai = jnp.broadcast_to(a_ref[s, :, half:], (n_batch, half))
        hr, hi = hs_ref[s, :, :half], hs_ref[s, :, half:]
        for t in range(tl):
            r0 = t * n_batch
            nr = ar * hr - ai * hi + x_ref[r0:r0 + n_batch, :half]
            ni = ar * hi + ai * hr + x_ref[r0:r0 + n_batch, half:]
            x_ref[r0:r0 + n_batch, :half] = nr
            x_ref[r0:r0 + n_batch, half:] = ni
            hr, hi = nr, ni
        hs_ref[s, :, :half] = hr
        hs_ref[s, :, half:] = hi

        y = (jnp.dot(x_ref[:, :half].astype(BF16), wc_ref[s, :half, :], preferred_element_type=F32)
             + jnp.dot(x_ref[:, half:].astype(BF16), wc_ref[s, half:, :], preferred_element_type=F32))
        g = jax.nn.gelu(y + d_ref[s] * u)
        g_ref[:, :, s * LANES:(s + 1) * LANES] = g.reshape(tl, n_batch, LANES).astype(g_ref.dtype)

    @pl.when(i == pl.num_programs(1) - 1)
    def _():
        hout_ref[...] = hs_ref[...]


def _s5(u3, h0, a_l, wb, wc, d_l, *, tl, ns):
    seq, n_batch, width = u3.shape
    n_slabs = width // LANES
    st = a_l.shape[-1]
    kern = functools.partial(_s5_kernel, tl=tl, n_batch=n_batch, ns=ns)
    return pl.pallas_call(
        kern,
        grid=(n_slabs // ns, seq // tl),
        in_specs=[
            pl.BlockSpec((tl, n_batch, ns * LANES), lambda j, i: (i, 0, j)),
            pl.BlockSpec((ns, n_batch, st), lambda j, i: (j, 0, 0)),
            pl.BlockSpec((ns, 1, st), lambda j, i: (j, 0, 0)),
            pl.BlockSpec((ns, LANES, st), lambda j, i: (j, 0, 0)),
            pl.BlockSpec((ns, st, LANES), lambda j, i: (j, 0, 0)),
            pl.BlockSpec((ns, 1, LANES), lambda j, i: (j, 0, 0)),
        ],
        out_specs=[
            pl.BlockSpec((tl, n_batch, ns * LANES), lambda j, i: (i, 0, j)),
            pl.BlockSpec((ns, n_batch, st), lambda j, i: (j, 0, 0)),
        ],
        out_shape=[
            jax.ShapeDtypeStruct((seq, n_batch, width), BF16),
            jax.ShapeDtypeStruct((n_slabs, n_batch, st), F32),
        ],
        scratch_shapes=[pltpu.VMEM((tl * n_batch, st), F32) for _ in range(ns)] + [pltpu.VMEM((ns, n_batch, st), F32)],
        compiler_params=_cparams(2),
        name="s5",
    )(u3, h0, a_l, wb, wc, d_l)


def _merge_kernel(x_ref, g_ref, attn_ref, gmix_ref, wga_ref, wgb_ref, wa_ref, wb_ref, wo_ref,
                  h_ref, xn_ref, acc_ref):
    j = pl.program_id(1)

    @pl.when(j == 0)
    def _():
        xn_ref[...] = (_rms(x_ref[...]) * gmix_ref[...]).astype(BF16)
        acc_ref[...] = jnp.zeros(acc_ref.shape, F32)

    xn = xn_ref[...]
    width = wa_ref.shape[0]
    n_seqs = g_ref.shape[1] // width
    g = jnp.concatenate([g_ref[:, b * width:(b + 1) * width] for b in range(n_seqs)], axis=0)
    ga = jax.nn.sigmoid(jnp.dot(xn, wga_ref[...], preferred_element_type=F32))
    gb = jax.nn.sigmoid(jnp.dot(xn, wgb_ref[...], preferred_element_type=F32))
    s5o = (jnp.dot(g, wa_ref[...], preferred_element_type=F32)
           * jax.nn.sigmoid(jnp.dot(g, wb_ref[...], preferred_element_type=F32)))
    attn = jnp.concatenate([attn_ref[h].astype(F32) for h in range(attn_ref.shape[0])], axis=1)
    mix = (ga * attn + gb * s5o).astype(BF16)
    acc_ref[...] += jnp.dot(mix, wo_ref[...], preferred_element_type=F32)

    @pl.when(j == pl.num_programs(1) - 1)
    def _():
        h_ref[...] = x_ref[...] + acc_ref[...]


def _merge(x2, g2, attn, gmix, wga, wgb, wa, wb, wo, *, n_batch, tile, cb):
    t_rows, d = x2.shape
    seq = t_rows // n_batch
    width = wa.shape[0]
    hpb = cb // LANES
    if tile <= seq:
        tps = seq // tile
        g_spec = pl.BlockSpec((tile, width), lambda r, j: (r % tps, r // tps))
    else:
        assert tile % seq == 0
        g_spec = pl.BlockSpec((seq, (tile // seq) * width), lambda r, j: (0, r))
    return pl.pallas_call(
        _merge_kernel,
        grid=(t_rows // tile, d // cb),
        in_specs=[
            pl.BlockSpec((tile, d), lambda r, j: (r, 0)),
            g_spec,
            pl.BlockSpec((hpb, tile, LANES), lambda r, j: (j, r, 0)),
            _const_spec(gmix.shape),
            pl.BlockSpec((d, cb), lambda r, j: (0, j)),
            pl.BlockSpec((d, cb), lambda r, j: (0, j)),
            pl.BlockSpec((d, cb), lambda r, j: (0, j)),
            pl.BlockSpec((d, cb), lambda r, j: (0, j)),
            pl.BlockSpec((cb, d), lambda r, j: (j, 0)),
        ],
        out_specs=pl.BlockSpec((tile, d), lambda r, j: (r, 0)),
        out_shape=jax.ShapeDtypeStruct((t_rows, d), F32),
        scratch_shapes=[pltpu.VMEM((tile, d), BF16), pltpu.VMEM((tile, d), F32)],
        compiler_params=_cparams(2),
        name="merge",
    )(x2, g2, attn, gmix, wga, wgb, wa, wb, wo)


def _mlp_kernel(h_ref, gmlp_ref, wup_ref, wdn_ref, gfin_ref, y_ref, hn_ref, acc_ref, *, final_norm):
    j = pl.program_id(1)

    @pl.when(j == 0)
    def _():
        hn_ref[...] = (_rms(h_ref[...]) * gmlp_ref[...]).astype(BF16)
        acc_ref[...] = jnp.zeros(acc_ref.shape, F32)

    a = jnp.maximum(jnp.dot(hn_ref[...], wup_ref[...], preferred_element_type=F32), 0.0)
    acc_ref[...] += jnp.dot((a * a).astype(BF16), wdn_ref[...], preferred_element_type=F32)

    @pl.when(j == pl.num_programs(1) - 1)
    def _():
        y = h_ref[...] + acc_ref[...]
        y_ref[...] = _rms(y) * gfin_ref[...] if final_norm else y


def _mlp(h2, gmlp, wup, wdn, gfin, *, tile, fb, final_norm):
    t_rows, d = h2.shape
    d_ff = wup.shape[1]
    return pl.pallas_call(
        functools.partial(_mlp_kernel, final_norm=final_norm),
        grid=(t_rows // tile, d_ff // fb),
        in_specs=[
            pl.BlockSpec((tile, d), lambda r, j: (r, 0)),
            _const_spec(gmlp.shape),
            pl.BlockSpec((d, fb), lambda r, j: (0, j)),
            pl.BlockSpec((fb, d), lambda r, j: (j, 0)),
            _const_spec(gfin.shape),
        ],
        out_specs=pl.BlockSpec((tile, d), lambda r, j: (r, 0)),
        out_shape=jax.ShapeDtypeStruct((t_rows, d), F32),
        scratch_shapes=[pltpu.VMEM((tile, d), BF16), pltpu.VMEM((tile, d), F32)],
        compiler_params=_cparams(2),
        name="mlp",
    )(h2, gmlp, wup, wdn, gfin)


def _rope_table(pos, rope):
    half = rope // 2
    inv = ROPE_THETA ** (-jnp.arange(half, dtype=F32) / half)
    ang = pos.astype(F32)[:, None] * inv[None, :]
    cos, sin = jnp.cos(ang), jnp.sin(ang)
    return jnp.concatenate([cos, cos, cos, cos, -sin, sin, -sin, sin, cos, cos, -sin, sin], axis=1)


def _swap_halves(w):
    half = w.shape[-1] // 2
    return jnp.concatenate([w[..., half:], w[..., :half]], axis=-1)


def _tiles(n_batch, seq, causal):
    rows = n_batch * seq
    t = dict(
        proj=_pick_tile(seq, PROJ_ROWS, 16) if seq >= PROJ_ROWS else rows,
        merge=_pick_tile(rows, MERGE_ROWS, 16),
        mlp=_pick_tile(rows, MLP_ROWS, 16),
        s5_tl=_pick_tile(seq, S5_TL, SUBLANES),
        attn=None, chains=1,
    )
    if causal:
        t["attn"] = _pick_tile(seq, ATTN_TILE, CHUNK)
        t["chains"] = max(c for c in range(1, ATTN_CHAINS + 1) if (seq // t["attn"]) % c == 0)
    return t


def _group(x, pos0, w, s5w, h0, cache):
    n_batch, seq, d = x.shape
    t_rows = n_batch * seq
    n_heads, kv_lora, rope = w["n_heads"], w["kv_lora"], w["rope"]
    x2 = x.reshape(t_rows, d)
    tiles = _tiles(n_batch, seq, causal=cache is None)
    row_tile = tiles["proj"]

    pos = pos0 + jnp.arange(seq)
    tab = _rope_table(pos, rope)
    if row_tile > seq:
        assert row_tile % seq == 0
        tab = jnp.tile(tab, (row_tile // seq, 1))
        tab_blocks = 1
    else:
        assert seq % row_tile == 0
        tab_blocks = seq // row_tile

    causal = cache is None
    outs = _proj(x2, tab, tab_blocks, w["gmix"], w["wlat"], w["gq"], w["wuqn"], w["wuqr"], w["gkv"], w["wu"],
                 (w["wuk"], w["wuv"]) if causal else None, n_batch=n_batch, tile=row_tile, n_heads=n_heads,
                 q_lora=w["q_lora"], kv_lora=kv_lora, rope=rope, scale=w["scale"])
    q, ckv, kr, u = outs[:4]
    ckv3 = ckv.reshape(n_batch, seq, kv_lora)
    kr3 = kr.reshape(n_batch, seq, rope)
    if causal:
        attn = _attention(q, outs[4], outs[5], n_batch=n_batch, seq=seq, tile=tiles["attn"], n_chains=tiles["chains"])
    else:
        attn = _attention_latent(q, cache[0].astype(F32), ckv3, cache[1].astype(F32), kr3, w["wuk"], w["wuv"],
                                 q_pos0=pos0)

    n_slabs = s5w["a"].shape[0]
    g, h_fin = _s5(u.reshape(seq, n_batch, -1), h0, s5w["a"], s5w["wb"], s5w["wc"], s5w["d"], tl=tiles["s5_tl"],
                   ns=math.gcd(n_slabs, S5_SLABS))

    h = _merge(x2, g.reshape(seq, -1), attn, w["gmix"], w["wga"], w["wgb"], w["wglu_a"], w["wglu_b"], w["wo"],
               n_batch=n_batch, tile=tiles["merge"], cb=4 * LANES)
    return h, ckv3, kr3, h_fin, tiles["mlp"]


def _state_to_slabs(s_re, s_im, n_slabs):
    n_batch, g, p = s_re.shape
    f = lambda s: s.astype(F32).reshape(n_batch, n_slabs, (g // n_slabs) * p).swapaxes(0, 1)
    return jnp.concatenate([f(s_re), f(s_im)], axis=-1)


def _slabs_to_state(h, g, p):
    n_slabs, n_batch, st = h.shape
    half = st // 2
    f = lambda s: s.swapaxes(0, 1).reshape(n_batch, g, p)
    return f(h[..., :half]), f(h[..., half:])


def kernel(x_prompt, x_sample, cache_ckv, cache_krope, state_s5_re, state_s5_im, norm_mix, w_in, norm_q, w_uq, norm_kv, w_uk, w_uv, s5_a_re, s5_a_im, s5_log_dt, s5_b_re, s5_b_im, s5_c_re, s5_c_im, s5_d, w_glu_a, w_glu_b, w_o, norm_mlp, w_up, w_down, norm_final):
    depth = w_in.shape[0]
    d = x_prompt.shape[-1]
    q_lora, kv_lora = norm_q.shape[1], norm_kv.shape[1]
    n_heads, nope = w_uk.shape[2], w_uk.shape[3]
    v_dim = w_uv.shape[3]
    rope = cache_krope.shape[-1]
    n_groups, n_state, grp = s5_b_re.shape[1:]
    width = n_groups * grp
    past_len = cache_ckv.shape[2]
    sdt = state_s5_re.dtype
    assert nope == LANES and v_dim == LANES and 2 * rope == LANES and d % (4 * LANES) == 0
    assert width % LANES == 0 and LANES % grp == 0 and x_prompt.shape[0] == SUBLANES == x_sample.shape[0]
    n_slabs = width // LANES
    gps = n_groups // n_slabs
    half = gps * n_state
    scale = math.log2(math.e) / math.sqrt(nope + rope)
    c1 = q_lora + kv_lora
    c2 = c1 + rope
    c3 = c2 + width
    c4 = c3 + d

    hp, hs = x_prompt, x_sample
    outs_p, outs_s = [], []
    for l in range(depth):
        wi = w_in[l]
        w_kr = wi[:, c1:c2]
        wuq3 = w_uq[l].reshape(q_lora, n_heads, nope + rope)
        w = dict(
            n_heads=n_heads, q_lora=q_lora, kv_lora=kv_lora, rope=rope, scale=scale,
            gmix=norm_mix[l][None].astype(F32), gq=norm_q[l][None].astype(F32), gkv=norm_kv[l][None].astype(F32),
            wlat=jnp.concatenate([wi[:, :c1], w_kr, _swap_halves(w_kr)], axis=1).astype(BF16),
            wuqn=wuq3[..., :nope].reshape(q_lora, n_heads * nope).astype(BF16),
            wuqr=wuq3[..., nope:].reshape(q_lora, n_heads * rope).astype(BF16),
            wu=wi[:, c2:c3].astype(BF16), wga=wi[:, c3:c4].astype(BF16), wgb=wi[:, c4:].astype(BF16),
            wuk=w_uk[l].reshape(kv_lora, n_heads * nope).astype(BF16),
            wuv=w_uv[l].reshape(kv_lora, n_heads * v_dim).astype(BF16),
            wglu_a=w_glu_a[l].astype(BF16), wglu_b=w_glu_b[l].astype(BF16), wo=w_o[l].astype(BF16),
        )
        abr, abi, bbr, bbi = _s5_discretize(s5_a_re[l], s5_a_im[l], s5_log_dt[l], s5_b_re[l], s5_b_im[l])
        on_diag = jnp.eye(gps, dtype=jnp.bool_)[None, :, None, :, None]
        bd_in = lambda b: jnp.where(
            on_diag, b.reshape(n_slabs, gps, n_state, grp).swapaxes(2, 3)[:, :, :, None, :], 0.0).reshape(
                n_slabs, LANES, half)
        bd_out = lambda c: jnp.where(
            on_diag, c.astype(F32).reshape(n_slabs, gps, grp, n_state).swapaxes(2, 3)[:, :, :, None, :], 0.0).reshape(
                n_slabs, half, LANES)
        s5w = dict(
            a=jnp.concatenate([abr.reshape(n_slabs, 1, half), abi.reshape(n_slabs, 1, half)], axis=-1),
            wb=jnp.concatenate([bd_in(bbr), bd_in(bbi)], axis=-1).astype(BF16),
            wc=jnp.concatenate([bd_out(s5_c_re[l]), -bd_out(s5_c_im[l])], axis=1).astype(BF16),
            d=s5_d[l].astype(F32).reshape(n_slabs, 1, LANES),
        )
        gmlp, gfin = norm_mlp[l][None].astype(F32), norm_final[None].astype(F32)
        wup, wdn = w_up[l].astype(BF16), w_down[l].astype(BF16)

        bp, lp, _ = hp.shape
        h0 = jnp.zeros((n_slabs, bp, 2 * half), F32)
        last = l == depth - 1
        h, ckv, kr, hfin, mlp_tile = _group(hp, 0, w, s5w, h0, None)
        outs_p.append((ckv, kr) + _slabs_to_state(hfin, n_groups, n_state))
        hp = _mlp(h, gmlp, wup, wdn, gfin, tile=mlp_tile, fb=FF_BLOCK, final_norm=last).reshape(bp, lp, d)

        bs, ls, _ = hs.shape
        h0 = _state_to_slabs(state_s5_re[l], state_s5_im[l], n_slabs)
        h, ckv, kr, hfin, mlp_tile = _group(hs, past_len, w, s5w, h0, (cache_ckv[l], cache_krope[l]))
        outs_s.append((ckv, kr) + _slabs_to_state(hfin, n_groups, n_state))
        hs = _mlp(h, gmlp, wup, wdn, gfin, tile=mlp_tile, fb=FF_BLOCK, final_norm=last).reshape(bs, ls, d)

    def stack(outs, i, dt):
        if len(outs) == 1:
            return outs[0][i].astype(dt)[None]
        return jnp.stack([o[i].astype(dt) for o in outs], axis=0)
    return (hp, hs,
            stack(outs_p, 0, F32), stack(outs_p, 1, F32), stack(outs_p, 2, sdt), stack(outs_p, 3, sdt),
            stack(outs_s, 0, F32), stack(outs_s, 1, F32), stack(outs_s, 2, sdt), stack(outs_s, 3, sdt))
```

```python
import functools
import math

import jax
import jax.numpy as jnp
from jax import lax
from jax.experimental import pallas as pl
from jax.experimental.pallas import tpu as pltpu

F32 = jnp.float32
BF16 = jnp.bfloat16

EPS = 1e-6
CHUNK = 64
ROPE_THETA = 10000.0
LANES = 128
SUBLANES = 8
VMEM_LIMIT = 56 * 1024 * 1024

PROJ_ROWS = 512
KV_ROWS = 1024
MERGE_ROWS = 512
MLP_ROWS = 512
FF_BLOCK = 1024
ATTN_TILE = 512
ATTN_CHAINS = 8
S5_TL = 128
S5_SLABS = 4


def _cparams(n_axes):
    return pltpu.CompilerParams(dimension_semantics=("arbitrary",) * n_axes,
                                vmem_limit_bytes=VMEM_LIMIT)


def _const_spec(shape):
    nd = len(shape)
    return pl.BlockSpec(shape, lambda *_: (0,) * nd, pipeline_mode=pl.Buffered(1))


def _rms(x):
    return x * lax.rsqrt(jnp.mean(x * x, axis=-1, keepdims=True) + EPS)


def _pick_tile(n, cap, mult):
    best = None
    for t in range(mult, min(n, cap) + 1, mult):
        if n % t == 0:
            best = t
    assert best is not None, (n, cap, mult)
    return best


def _disc_kernel(ar_ref, ai_ref, ldt_ref, br_ref, bi_ref, abr_ref, abi_ref, bbr_ref, bbi_ref):
    ar = ar_ref[...]
    ai = ai_ref[...]
    dt = jnp.exp(ldt_ref[...])
    mag = jnp.exp(ar * dt)
    abar_re = mag * jnp.cos(ai * dt)
    abar_im = mag * jnp.sin(ai * dt)
    den = ar * ar + ai * ai
    nr = abar_re - 1.0
    coef_re = (nr * ar + abar_im * ai) / den
    coef_im = (abar_im * ar - nr * ai) / den
    br = br_ref[...]
    bi = bi_ref[...]
    abr_ref[...] = abar_re
    abi_ref[...] = abar_im
    bbr_ref[...] = coef_re * br - coef_im * bi
    bbi_ref[...] = coef_re * bi + coef_im * br


def _s5_discretize(a_re, a_im, log_dt, b_re, b_im):
    g, p, m = b_re.shape
    rep = lambda a: jnp.repeat(a.astype(F32), m, axis=1)
    flat = lambda b: b.astype(F32).reshape(g, p * m)
    shp = jax.ShapeDtypeStruct((g, p * m), F32)
    abr, abi, bbr, bbi = pl.pallas_call(
        _disc_kernel, out_shape=(shp, shp, shp, shp), name="s5_disc",
    )(rep(a_re), rep(a_im), rep(log_dt), flat(b_re), flat(b_im))
    return abr[:, ::m], abi[:, ::m], bbr.reshape(g, p, m), bbi.reshape(g, p, m)


def _proj_kernel(x_ref, tab_ref, gmix_ref, wlat_ref, gq_ref, wuqn_ref, wuqr_ref, gkv_ref, wu_ref,
                 q_ref, ckv_ref, kr_ref, u_ref, *, q_lora, kv_lora, n_heads, scale):
    x = x_ref[...]
    xn = (_rms(x) * gmix_ref[...]).astype(BF16)
    z = jnp.dot(xn, wlat_ref[...], preferred_element_type=F32)
    cqn = (_rms(z[:, :q_lora]) * gq_ref[...]).astype(BF16)
    cos2, sin2, tab_k = tab_ref[:, :LANES], tab_ref[:, LANES:2 * LANES], tab_ref[:, 2 * LANES:]
    first_half = (lax.broadcasted_iota(jnp.int32, cos2.shape, 1) % 64) < 32
    for j in range(n_heads // 2):
        qn = (jnp.dot(cqn, wuqn_ref[:, j * 256:(j + 1) * 256], preferred_element_type=F32) * scale).astype(BF16)
        q_ref[2 * j, :, 0:LANES] = qn[:, :LANES]
        q_ref[2 * j + 1, :, 0:LANES] = qn[:, LANES:]
    for j in range(n_heads // 4):
        qr = jnp.dot(cqn, wuqr_ref[:, j * 256:(j + 1) * 256], preferred_element_type=F32) * scale
        for e in range(2):
            xr = qr[:, e * LANES:(e + 1) * LANES]
            swapped = jnp.where(first_half, pltpu.roll(xr, 96, 1), pltpu.roll(xr, 32, 1))
            rot = xr * cos2 + swapped * sin2
            ha = 4 * j + 2 * e
            q_ref[ha, :, LANES:2 * LANES] = rot.astype(BF16)
            q_ref[ha + 1, :, LANES:2 * LANES] = pltpu.roll(rot, 64, 1).astype(BF16)
    ckv_ref[...] = _rms(z[:, q_lora:q_lora + kv_lora]) * gkv_ref[...]
    kr = z[:, q_lora + kv_lora:] * tab_k
    kr = kr + pltpu.roll(kr, 64, 1)
    kr_ref[...] = kr[:, :64]
    u_ref[...] = jnp.dot(xn, wu_ref[...], preferred_element_type=F32)


def _proj(x2, tab, tab_blocks, gmix, wlat, gq, wuqn, wuqr, gkv, wu, *, tile, n_heads, q_lora, kv_lora, rope, scale):
    t_rows, d = x2.shape
    n_tiles = t_rows // tile
    kern = functools.partial(_proj_kernel, q_lora=q_lora, kv_lora=kv_lora, n_heads=n_heads, scale=scale)
    return pl.pallas_call(
        kern,
        grid=(n_tiles,),
        in_specs=[
            pl.BlockSpec((tile, d), lambda r: (r, 0)),
            pl.BlockSpec((tile, tab.shape[1]), lambda r: (r % tab_blocks, 0)),
            _const_spec(gmix.shape), _const_spec(wlat.shape), _const_spec(gq.shape),
            _const_spec(wuqn.shape), _const_spec(wuqr.shape), _const_spec(gkv.shape), _const_spec(wu.shape),
        ],
        out_specs=[
            pl.BlockSpec((n_heads, tile, 2 * LANES), lambda r: (0, r, 0)),
            pl.BlockSpec((tile, kv_lora), lambda r: (r, 0)),
            pl.BlockSpec((tile, rope), lambda r: (r, 0)),
            pl.BlockSpec((tile, wu.shape[1]), lambda r: (r, 0)),
        ],
        out_shape=[
            jax.ShapeDtypeStruct((n_heads, t_rows, 2 * LANES), BF16),
            jax.ShapeDtypeStruct((t_rows, kv_lora), F32),
            jax.ShapeDtypeStruct((t_rows, rope), F32),
            jax.ShapeDtypeStruct((t_rows, wu.shape[1]), F32),
        ],
        compiler_params=_cparams(1),
        name="proj",
    )(x2, tab, gmix, wlat, gq, wuqn, wuqr, gkv, wu)


def _kvup_kernel(ckv_ref, kr_ref, wuk_ref, wuv_ref, k_ref, v_ref, *, n_heads):
    c = ckv_ref[...].astype(BF16)
    kr = kr_ref[...]
    krz = jnp.concatenate([kr, jnp.zeros_like(kr)], axis=1).astype(BF16)
    for hp in range(n_heads // 2):
        kn = jnp.dot(c, wuk_ref[:, hp * 256:(hp + 1) * 256], preferred_element_type=F32).astype(BF16)
        vv = jnp.dot(c, wuv_ref[:, hp * 256:(hp + 1) * 256], preferred_element_type=F32).astype(BF16)
        for s in range(2):
            h = 2 * hp + s
            k_ref[h, :, 0:LANES] = kn[:, s * LANES:(s + 1) * LANES]
            k_ref[h, :, LANES:2 * LANES] = krz
            v_ref[h] = vv[:, s * LANES:(s + 1) * LANES]


def _kvup(ckv2, kr2, wuk, wuv, *, tile, n_heads):
    t_rows, kv_lora = ckv2.shape
    rope = kr2.shape[1]
    kern = functools.partial(_kvup_kernel, n_heads=n_heads)
    return pl.pallas_call(
        kern,
        grid=(t_rows // tile,),
        in_specs=[
            pl.BlockSpec((tile, kv_lora), lambda r: (r, 0)),
            pl.BlockSpec((tile, rope), lambda r: (r, 0)),
            _const_spec(wuk.shape), _const_spec(wuv.shape),
        ],
        out_specs=[
            pl.BlockSpec((n_heads, tile, 2 * LANES), lambda r: (0, r, 0)),
            pl.BlockSpec((n_heads, tile, LANES), lambda r: (0, r, 0)),
        ],
        out_shape=[
            jax.ShapeDtypeStruct((n_heads, t_rows, 2 * LANES), BF16),
            jax.ShapeDtypeStruct((n_heads, t_rows, LANES), BF16),
        ],
        compiler_params=_cparams(1),
        name="kvup",
    )(ckv2, kr2, wuk, wuv)


def _attn_tile(q, k, v1, m_ref, acc_ref, mask):
    s = lax.dot_general(q, k, (((1,), (1,)), ((), ())), preferred_element_type=F32)
    if mask is not None:
        s = jnp.where(mask, s, -jnp.inf)
    cols = [s[:, c * LANES:(c + 1) * LANES] for c in range(s.shape[1] // LANES)]
    m_prev = m_ref[...]
    m_new = jnp.maximum(m_prev, jnp.max(functools.reduce(jnp.maximum, cols), axis=-1, keepdims=True))
    alpha = jnp.exp2(m_prev - m_new)
    p = jnp.concatenate([jnp.exp2(c - m_new).astype(BF16) for c in cols], axis=1)
    acc_ref[...] = (jnp.concatenate([alpha, alpha], axis=1) * acc_ref[...]
                    + jnp.dot(p, v1, preferred_element_type=F32))
    m_ref[...] = m_new


def _attn_init(m_ref, acc_ref):
    m_ref[...] = jnp.full(m_ref.shape, -jnp.inf, F32)
    acc_ref[...] = jnp.zeros(acc_ref.shape, F32)


def _attn_finish(acc_ref):
    acc = acc_ref[...]
    return acc[:, :LANES] / acc[:, LANES:]


def _chunk_mask(tq, tk, q_off, k_off):
    qc = (lax.broadcasted_iota(jnp.int32, (tq, tk), 0) + q_off) // CHUNK
    kc = (lax.broadcasted_iota(jnp.int32, (tq, tk), 1) + k_off) // CHUNK
    return kc <= qc


def _attn_causal_kernel(q_ref, k_ref, v_ref, o_ref, v1_ref, *stat_refs, tile, n_chains, n_qblocks):
    ms, accs = stat_refs[0::2], stat_refs[1::2]
    for c in range(n_chains):
        _attn_init(ms[c], accs[c])
    qs = [q_ref[c * tile:(c + 1) * tile, :] for c in range(n_chains)]
    v1_ref[:, 0:LANES] = v_ref[...]
    v1_ref[:, LANES:2 * LANES] = jnp.ones((v1_ref.shape[0], LANES), BF16)

    if n_qblocks > 1:
        qi = pl.program_id(2)

        def full_tile(j, carry):
            off = pl.multiple_of(j * tile, tile)
            k = k_ref[pl.ds(off, tile), :]
            v1 = v1_ref[pl.ds(off, tile), :]
            for c in range(n_chains):
                _attn_tile(qs[c], k, v1, ms[c], accs[c], None)
            return carry

        lax.fori_loop(0, qi * n_chains, full_tile, 0)
        base = qi * n_chains
    else:
        base = 0
    for d in range(n_chains):
        off = (base + d) * tile
        if n_qblocks > 1:
            off = pl.multiple_of(off, tile)
        k = k_ref[pl.ds(off, tile), :]
        v1 = v1_ref[pl.ds(off, tile), :]
        for c in range(d, n_chains):
            _attn_tile(qs[c], k, v1, ms[c], accs[c], _chunk_mask(tile, tile, 0, 0) if c == d else None)
    for c in range(n_chains):
        o_ref[c * tile:(c + 1) * tile, :] = _attn_finish(accs[c]).astype(o_ref.dtype)


def _attention(q, k, v, *, n_batch, seq, tile, n_chains):
    n_heads = q.shape[0]
    tq = tile * n_chains
    nq = seq // tq
    assert tile % CHUNK == 0 and seq % tq == 0
    stats = [pltpu.VMEM((seq, 2 * LANES), BF16)]
    for _ in range(n_chains):
        stats += [pltpu.VMEM((tile, LANES), F32), pltpu.VMEM((tile, 2 * LANES), F32)]
    return pl.pallas_call(
        functools.partial(_attn_causal_kernel, tile=tile, n_chains=n_chains, n_qblocks=nq),
        grid=(n_batch, n_heads, nq),
        in_specs=[
            pl.BlockSpec((None, tq, 2 * LANES), lambda b, h, i: (h, b * nq + i, 0)),
            pl.BlockSpec((None, seq, 2 * LANES), lambda b, h, i: (h, b, 0)),
            pl.BlockSpec((None, seq, LANES), lambda b, h, i: (h, b, 0)),
        ],
        out_specs=pl.BlockSpec((None, tq, LANES), lambda b, h, i: (h, b * nq + i, 0)),
        out_shape=jax.ShapeDtypeStruct((n_heads, n_batch * seq, LANES), BF16),
        scratch_shapes=stats,
        compiler_params=_cparams(3),
        name="attn_causal",
    )(q, k, v)


def _attn_latent_kernel(q_ref, cc_ref, cn_ref, kc_ref, kn_ref, wuk_ref, wuv_ref, o_ref, *, q_pos0, n_heads):
    ls = q_ref.shape[1]
    nt = (((1,), (1,)), ((), ()))
    ckv = jnp.concatenate([cc_ref[...], cn_ref[...]], axis=0).astype(BF16)
    kr = jnp.concatenate([kc_ref[...], kn_ref[...]], axis=0)
    krz = jnp.concatenate([kr, jnp.zeros_like(kr)], axis=1).astype(BF16)
    q_lat = jnp.concatenate(
        [lax.dot_general(q_ref[h, :, :LANES], wuk_ref[:, h * LANES:(h + 1) * LANES], nt, preferred_element_type=F32)
         for h in range(n_heads)], axis=0).astype(BF16)
    q_rope = jnp.concatenate([q_ref[h, :, LANES:] for h in range(n_heads)], axis=0)
    s = (lax.dot_general(q_lat, ckv, nt, preferred_element_type=F32)
         + lax.dot_general(q_rope, krz, nt, preferred_element_type=F32))
    rows, lk = s.shape
    qc = (lax.rem(lax.broadcasted_iota(jnp.int32, (rows, lk), 0), ls) + q_pos0) // CHUNK
    kc = lax.broadcasted_iota(jnp.int32, (rows, lk), 1) // CHUNK
    s = jnp.where(kc <= qc, s, -jnp.inf)
    p = jnp.exp2(s - jnp.max(s, axis=-1, keepdims=True))
    l = jnp.sum(p, axis=-1, keepdims=True)
    o_lat = (jnp.dot(p.astype(BF16), ckv, preferred_element_type=F32) / l).astype(BF16)
    for h in range(n_heads):
        o_ref[h] = jnp.dot(o_lat[h * ls:(h + 1) * ls, :], wuv_ref[:, h * LANES:(h + 1) * LANES],
                           preferred_element_type=F32).astype(o_ref.dtype)


def _attention_latent(q, ckv_cache, ckv_new, kr_cache, kr_new, wuk, wuv, *, q_pos0):
    n_heads = q.shape[0]
    n_batch, past, kv_lora = ckv_cache.shape
    ls, rope = ckv_new.shape[1], kr_new.shape[2]
    kern = functools.partial(_attn_latent_kernel, q_pos0=q_pos0, n_heads=n_heads)
    return pl.pallas_call(
        kern,
        grid=(n_batch,),
        in_specs=[
            pl.BlockSpec((n_heads, ls, 2 * LANES), lambda b: (0, b, 0)),
            pl.BlockSpec((None, past, kv_lora), lambda b: (b, 0, 0)),
            pl.BlockSpec((None, ls, kv_lora), lambda b: (b, 0, 0)),
            pl.BlockSpec((None, past, rope), lambda b: (b, 0, 0)),
            pl.BlockSpec((None, ls, rope), lambda b: (b, 0, 0)),
            _const_spec(wuk.shape), _const_spec(wuv.shape),
        ],
        out_specs=pl.BlockSpec((n_heads, ls, LANES), lambda b: (0, b, 0)),
        out_shape=jax.ShapeDtypeStruct((n_heads, n_batch * ls, LANES), BF16),
        compiler_params=_cparams(1),
        name="attn_latent",
    )(q, ckv_cache, ckv_new, kr_cache, kr_new, wuk, wuv)


def _s5_kernel(u_ref, h0_ref, a_ref, wb_ref, wc_ref, d_ref, g_ref, hout_ref, *scratch, tl, n_batch, ns):
    i = pl.program_id(1)
    half = a_ref.shape[-1] // 2
    rows = tl * n_batch
    x_refs, hs_ref = scratch[:ns], scratch[ns]

    @pl.when(i == 0)
    def _():
        hs_ref[...] = h0_ref[...]

    ut = jnp.swapaxes(u_ref[...], 0, 1).reshape(rows, ns * LANES)
    for s in range(ns):
        x_ref = x_refs[s]
        u = ut[:, s * LANES:(s + 1) * LANES]
        ub = u.astype(BF16)
        x_ref[:, :half] = jnp.dot(ub, wb_ref[s, :, :half], preferred_element_type=F32)
        x_ref[:, half:] = jnp.dot(ub, wb_ref[s, :, half:], preferred_element_type=F32)
        ar = jnp.broadcast_to(a_ref[s, :, :half], (n_batch, half))
        ai = jnp.broadcast_to(a_ref[s, :, half:], (n_batch, half))
        hr, hi = hs_ref[s, :, :half], hs_ref[s, :, half:]
        for t in range(tl):
            r0 = t * n_batch
            nr = ar * hr - ai * hi + x_ref[r0:r0 + n_batch, :half]
            ni = ar * hi + ai * hr + x_ref[r0:r0 + n_batch, half:]
            x_ref[r0:r0 + n_batch, :half] = nr
            x_ref[r0:r0 + n_batch, half:] = ni
            hr, hi = nr, ni
        hs_ref[s, :, :half] = hr
        hs_ref[s, :, half:] = hi

        y = (jnp.dot(x_ref[:, :half].astype(BF16), wc_ref[s, :half, :], preferred_element_type=F32)
             + jnp.dot(x_ref[:, half:].astype(BF16), wc_ref[s, half:, :], preferred_element_type=F32))
        g = jax.nn.gelu(y + d_ref[s] * u)
        g_ref[:, :, s * LANES:(s + 1) * LANES] = jnp.swapaxes(
            g.reshape(tl, n_batch, LANES), 0, 1).astype(g_ref.dtype)

    @pl.when(i == pl.num_programs(1) - 1)
    def _():
        hout_ref[...] = hs_ref[...]


def _s5(u3, h0, a_l, wb, wc, d_l, *, tl, ns):
    n_batch, seq, width = u3.shape
    n_slabs = width // LANES
    st = a_l.shape[-1]
    kern = functools.partial(_s5_kernel, tl=tl, n_batch=n_batch, ns=ns)
    return pl.pallas_call(
        kern,
        grid=(n_slabs // ns, seq // tl),
        in_specs=[
            pl.BlockSpec((n_batch, tl, ns * LANES), lambda j, i: (0, i, j)),
            pl.BlockSpec((ns, n_batch, st), lambda j, i: (j, 0, 0)),
            pl.BlockSpec((ns, 1, st), lambda j, i: (j, 0, 0)),
            pl.BlockSpec((ns, LANES, st), lambda j, i: (j, 0, 0)),
            pl.BlockSpec((ns, st, LANES), lambda j, i: (j, 0, 0)),
            pl.BlockSpec((ns, 1, LANES), lambda j, i: (j, 0, 0)),
        ],
        out_specs=[
            pl.BlockSpec((n_batch, tl, ns * LANES), lambda j, i: (0, i, j)),
            pl.BlockSpec((ns, n_batch, st), lambda j, i: (j, 0, 0)),
        ],
        out_shape=[
            jax.ShapeDtypeStruct((n_batch, seq, width), BF16),
            jax.ShapeDtypeStruct((n_slabs, n_batch, st), F32),
        ],
        scratch_shapes=[pltpu.VMEM((tl * n_batch, st), F32) for _ in range(ns)] + [pltpu.VMEM((ns, n_batch, st), F32)],
        compiler_params=_cparams(2),
        name="s5",
    )(u3, h0, a_l, wb, wc, d_l)


def _merge_kernel(x_ref, g_ref, attn_ref, gmix_ref, wga_ref, wgb_ref, wa_ref, wb_ref, wo_ref,
                  h_ref, xn_ref, acc_ref):
    j = pl.program_id(1)

    @pl.when(j == 0)
    def _():
        xn_ref[...] = (_rms(x_ref[...]) * gmix_ref[...]).astype(BF16)
        acc_ref[...] = jnp.zeros(acc_ref.shape, F32)

    xn = xn_ref[...]
    g = g_ref[...]
    ga = jax.nn.sigmoid(jnp.dot(xn, wga_ref[...], preferred_element_type=F32))
    gb = jax.nn.sigmoid(jnp.dot(xn, wgb_ref[...], preferred_element_type=F32))
    s5o = (jnp.dot(g, wa_ref[...], preferred_element_type=F32)
           * jax.nn.sigmoid(jnp.dot(g, wb_ref[...], preferred_element_type=F32)))
    attn = jnp.concatenate([attn_ref[h].astype(F32) for h in range(attn_ref.shape[0])], axis=1)
    mix = (ga * attn + gb * s5o).astype(BF16)
    acc_ref[...] += jnp.dot(mix, wo_ref[...], preferred_element_type=F32)

    @pl.when(j == pl.num_programs(1) - 1)
    def _():
        h_ref[...] = x_ref[...] + acc_ref[...]


def _merge(x2, g2, attn, gmix, wga, wgb, wa, wb, wo, *, tile, cb):
    t_rows, d = x2.shape
    hpb = cb // LANES
    return pl.pallas_call(
        _merge_kernel,
        grid=(t_rows // tile, d // cb),
        in_specs=[
            pl.BlockSpec((tile, d), lambda r, j: (r, 0)),
            pl.BlockSpec((tile, d), lambda r, j: (r, 0)),
            pl.BlockSpec((hpb, tile, LANES), lambda r, j: (j, r, 0)),
            _const_spec(gmix.shape),
            pl.BlockSpec((d, cb), lambda r, j: (0, j)),
            pl.BlockSpec((d, cb), lambda r, j: (0, j)),
            pl.BlockSpec((d, cb), lambda r, j: (0, j)),
            pl.BlockSpec((d, cb), lambda r, j: (0, j)),
            pl.BlockSpec((cb, d), lambda r, j: (j, 0)),
        ],
        out_specs=pl.BlockSpec((tile, d), lambda r, j: (r, 0)),
        out_shape=jax.ShapeDtypeStruct((t_rows, d), F32),
        scratch_shapes=[pltpu.VMEM((tile, d), BF16), pltpu.VMEM((tile, d), F32)],
        compiler_params=_cparams(2),
        name="merge",
    )(x2, g2, attn, gmix, wga, wgb, wa, wb, wo)


def _mlp_kernel(h_ref, gmlp_ref, wup_ref, wdn_ref, gfin_ref, y_ref, hn_ref, acc_ref, *, final_norm):
    j = pl.program_id(1)

    @pl.when(j == 0)
    def _():
        hn_ref[...] = (_rms(h_ref[...]) * gmlp_ref[...]).astype(BF16)
        acc_ref[...] = jnp.zeros(acc_ref.shape, F32)

    a = jnp.maximum(jnp.dot(hn_ref[...], wup_ref[...], preferred_element_type=F32), 0.0)
    acc_ref[...] += jnp.dot((a * a).astype(BF16), wdn_ref[...], preferred_element_type=F32)

    @pl.when(j == pl.num_programs(1) - 1)
    def _():
        y = h_ref[...] + acc_ref[...]
        y_ref[...] = _rms(y) * gfin_ref[...] if final_norm else y


def _mlp(h2, gmlp, wup, wdn, gfin, *, tile, fb, final_norm):
    t_rows, d = h2.shape
    d_ff = wup.shape[1]
    return pl.pallas_call(
        functools.partial(_mlp_kernel, final_norm=final_norm),
        grid=(t_rows // tile, d_ff // fb),
        in_specs=[
            pl.BlockSpec((tile, d), lambda r, j: (r, 0)),
            _const_spec(gmlp.shape),
            pl.BlockSpec((d, fb), lambda r, j: (0, j)),
            pl.BlockSpec((fb, d), lambda r, j: (j, 0)),
            _const_spec(gfin.shape),
        ],
        out_specs=pl.BlockSpec((tile, d), lambda r, j: (r, 0)),
        out_shape=jax.ShapeDtypeStruct((t_rows, d), F32),
        scratch_shapes=[pltpu.VMEM((tile, d), BF16), pltpu.VMEM((tile, d), F32)],
        compiler_params=_cparams(2),
        name="mlp",
    )(h2, gmlp, wup, wdn, gfin)


def _rope_table(pos, rope):
    half = rope // 2
    inv = ROPE_THETA ** (-jnp.arange(half, dtype=F32) / half)
    ang = pos.astype(F32)[:, None] * inv[None, :]
    cos, sin = jnp.cos(ang), jnp.sin(ang)
    return jnp.concatenate([cos, cos, cos, cos, -sin, sin, -sin, sin, cos, cos, -sin, sin], axis=1)


def _swap_halves(w):
    half = w.shape[-1] // 2
    return jnp.concatenate([w[..., half:], w[..., :half]], axis=-1)


def _tiles(n_batch, seq, causal):
    rows = n_batch * seq
    t = dict(
        proj=_pick_tile(seq, PROJ_ROWS, 16) if seq >= PROJ_ROWS else rows,
        merge=_pick_tile(rows, MERGE_ROWS, 16),
        mlp=_pick_tile(rows, MLP_ROWS, 16),
        s5_tl=_pick_tile(seq, S5_TL, SUBLANES),
        attn=None, chains=1,
    )
    if causal:
        t["attn"] = _pick_tile(seq, ATTN_TILE, CHUNK)
        t["chains"] = max(c for c in range(1, ATTN_CHAINS + 1) if (seq // t["attn"]) % c == 0)
    return t


def _group(x, pos0, w, s5w, h0, cache):
    n_batch, seq, d = x.shape
    t_rows = n_batch * seq
    n_heads, kv_lora, rope = w["n_heads"], w["kv_lora"], w["rope"]
    x2 = x.reshape(t_rows, d)
    tiles = _tiles(n_batch, seq, causal=cache is None)
    row_tile = tiles["proj"]

    pos = pos0 + jnp.arange(seq)
    tab = _rope_table(pos, rope)
    if row_tile > seq:
        assert row_tile % seq == 0
        tab = jnp.tile(tab, (row_tile // seq, 1))
        tab_blocks = 1
    else:
        assert seq % row_tile == 0
        tab_blocks = seq // row_tile

    q, ckv, kr, u = _proj(x2, tab, tab_blocks, w["gmix"], w["wlat"], w["gq"], w["wuqn"], w["wuqr"], w["gkv"], w["wu"],
                          tile=row_tile, n_heads=n_heads, q_lora=w["q_lora"], kv_lora=kv_lora, rope=rope,
                          scale=w["scale"])
    ckv3 = ckv.reshape(n_batch, seq, kv_lora)
    kr3 = kr.reshape(n_batch, seq, rope)
    if cache is not None:
        attn = _attention_latent(q, cache[0].astype(F32), ckv3, cache[1].astype(F32), kr3, w["wuk"], w["wuv"],
                                 q_pos0=pos0)
    else:
        k, v = _kvup(ckv, kr, w["wuk"], w["wuv"], tile=_pick_tile(t_rows, KV_ROWS, 16), n_heads=n_heads)
        attn = _attention(q, k, v, n_batch=n_batch, seq=seq, tile=tiles["attn"], n_chains=tiles["chains"])

    n_slabs = s5w["a"].shape[0]
    g, h_fin = _s5(u.reshape(n_batch, seq, -1), h0, s5w["a"], s5w["wb"], s5w["wc"], s5w["d"], tl=tiles["s5_tl"],
                   ns=math.gcd(n_slabs, S5_SLABS))

    h = _merge(x2, g.reshape(t_rows, -1), attn, w["gmix"], w["wga"], w["wgb"], w["wglu_a"], w["wglu_b"], w["wo"],
               tile=tiles["merge"], cb=4 * LANES)
    return h, ckv3, kr3, h_fin, tiles["mlp"]


def _state_to_slabs(s_re, s_im, n_slabs):
    n_batch, g, p = s_re.shape
    f = lambda s: s.astype(F32).reshape(n_batch, n_slabs, (g // n_slabs) * p).swapaxes(0, 1)
    return jnp.concatenate([f(s_re), f(s_im)], axis=-1)


def _slabs_to_state(h, g, p):
    n_slabs, n_batch, st = h.shape
    half = st // 2
    f = lambda s: s.swapaxes(0, 1).reshape(n_batch, g, p)
    return f(h[..., :half]), f(h[..., half:])


def kernel(x_prompt, x_sample, cache_ckv, cache_krope, state_s5_re, state_s5_im, norm_mix, w_in, norm_q, w_uq, norm_kv, w_uk, w_uv, s5_a_re, s5_a_im, s5_log_dt, s5_b_re, s5_b_im, s5_c_re, s5_c_im, s5_d, w_glu_a, w_glu_b, w_o, norm_mlp, w_up, w_down, norm_final):
    depth = w_in.shape[0]
    d = x_prompt.shape[-1]
    q_lora, kv_lora = norm_q.shape[1], norm_kv.shape[1]
    n_heads, nope = w_uk.shape[2], w_uk.shape[3]
    v_dim = w_uv.shape[3]
    rope = cache_krope.shape[-1]
    n_groups, n_state, grp = s5_b_re.shape[1:]
    width = n_groups * grp
    past_len = cache_ckv.shape[2]
    sdt = state_s5_re.dtype
    assert nope == LANES and v_dim == LANES and 2 * rope == LANES and d % (4 * LANES) == 0
    assert width % LANES == 0 and LANES % grp == 0 and x_prompt.shape[0] == SUBLANES == x_sample.shape[0]
    n_slabs = width // LANES
    gps = n_groups // n_slabs
    half = gps * n_state
    scale = math.log2(math.e) / math.sqrt(nope + rope)
    c1 = q_lora + kv_lora
    c2 = c1 + rope
    c3 = c2 + width
    c4 = c3 + d

    hp, hs = x_prompt, x_sample
    outs_p, outs_s = [], []
    for l in range(depth):
        wi = w_in[l]
        w_kr = wi[:, c1:c2]
        wuq3 = w_uq[l].reshape(q_lora, n_heads, nope + rope)
        w = dict(
            n_heads=n_heads, q_lora=q_lora, kv_lora=kv_lora, rope=rope, scale=scale,
            gmix=norm_mix[l][None].astype(F32), gq=norm_q[l][None].astype(F32), gkv=norm_kv[l][None].astype(F32),
            wlat=jnp.concatenate([wi[:, :c1], w_kr, _swap_halves(w_kr)], axis=1).astype(BF16),
            wuqn=wuq3[..., :nope].reshape(q_lora, n_heads * nope).astype(BF16),
            wuqr=wuq3[..., nope:].reshape(q_lora, n_heads * rope).astype(BF16),
            wu=wi[:, c2:c3].astype(BF16), wga=wi[:, c3:c4].astype(BF16), wgb=wi[:, c4:].astype(BF16),
            wuk=w_uk[l].reshape(kv_lora, n_heads * nope).astype(BF16),
            wuv=w_uv[l].reshape(kv_lora, n_heads * v_dim).astype(BF16),
            wglu_a=w_glu_a[l].astype(BF16), wglu_b=w_glu_b[l].astype(BF16), wo=w_o[l].astype(BF16),
        )
        abr, abi, bbr, bbi = _s5_discretize(s5_a_re[l], s5_a_im[l], s5_log_dt[l], s5_b_re[l], s5_b_im[l])
        on_diag = jnp.eye(gps, dtype=jnp.bool_)[None, :, None, :, None]
        bd_in = lambda b: jnp.where(
            on_diag, b.reshape(n_slabs, gps, n_state, grp).swapaxes(2, 3)[:, :, :, None, :], 0.0).reshape(
                n_slabs, LANES, half)
        bd_out = lambda c: jnp.where(
            on_diag, c.astype(F32).reshape(n_slabs, gps, grp, n_state).swapaxes(2, 3)[:, :, :, None, :], 0.0).reshape(
                n_slabs, half, LANES)
        s5w = dict(
            a=jnp.concatenate([abr.reshape(n_slabs, 1, half), abi.reshape(n_slabs, 1, half)], axis=-1),
            wb=jnp.concatenate([bd_in(bbr), bd_in(bbi)], axis=-1).astype(BF16),
            wc=jnp.concatenate([bd_out(s5_c_re[l]), -bd_out(s5_c_im[l])], axis=1).astype(BF16),
            d=s5_d[l].astype(F32).reshape(n_slabs, 1, LANES),
        )
        gmlp, gfin = norm_mlp[l][None].astype(F32), norm_final[None].astype(F32)
        wup, wdn = w_up[l].astype(BF16), w_down[l].astype(BF16)

        bp, lp, _ = hp.shape
        h0 = jnp.zeros((n_slabs, bp, 2 * half), F32)
        last = l == depth - 1
        h, ckv, kr, hfin, mlp_tile = _group(hp, 0, w, s5w, h0, None)
        outs_p.append((ckv, kr) + _slabs_to_state(hfin, n_groups, n_state))
        hp = _mlp(h, gmlp, wup, wdn, gfin, tile=mlp_tile, fb=FF_BLOCK, final_norm=last).reshape(bp, lp, d)

        bs, ls, _ = hs.shape
        h0 = _state_to_slabs(state_s5_re[l], state_s5_im[l], n_slabs)
        h, ckv, kr, hfin, mlp_tile = _group(hs, past_len, w, s5w, h0, (cache_ckv[l], cache_krope[l]))
        outs_s.append((ckv, kr) + _slabs_to_state(hfin, n_groups, n_state))
        hs = _mlp(h, gmlp, wup, wdn, gfin, tile=mlp_tile, fb=FF_BLOCK, final_norm=last).reshape(bs, ls, d)

    def stack(outs, i, dt):
        if len(outs) == 1:
            return outs[0][i].astype(dt)[None]
        return jnp.stack([o[i].astype(dt) for o in outs], axis=0)
    return (hp, hs,
            stack(outs_p, 0, F32), stack(outs_p, 1, F32), stack(outs_p, 2, sdt), stack(outs_p, 3, sdt),
            stack(outs_s, 0, F32), stack(outs_s, 1, F32), stack(outs_s, 2, sdt), stack(outs_s, 3, sdt))
```

```python
import functools
import math

import jax
import jax.numpy as jnp
from jax import lax
from jax.experimental import pallas as pl
from jax.experimental.pallas import tpu as pltpu

F32 = jnp.float32
BF16 = jnp.bfloat16

EPS = 1e-6
CHUNK = 64
ROPE_THETA = 10000.0
LANES = 128
SUBLANES = 8
VMEM_LIMIT = 56 * 1024 * 1024

PROJ_ROWS = 512
KV_ROWS = 1024
MERGE_ROWS = 512
MLP_ROWS = 512
FF_BLOCK = 1024
ATTN_TILE = 512
ATTN_CHAINS = 8
S5_TL = 128
S5_SLABS = 4


def _cparams(n_axes):
    return pltpu.CompilerParams(dimension_semantics=("arbitrary",) * n_axes,
                                vmem_limit_bytes=VMEM_LIMIT)


def _const_spec(shape):
    nd = len(shape)
    return pl.BlockSpec(shape, lambda *_: (0,) * nd, pipeline_mode=pl.Buffered(1))


def _rms(x):
    return x * lax.rsqrt(jnp.mean(x * x, axis=-1, keepdims=True) + EPS)


def _pick_tile(n, cap, mult):
    best = None
    for t in range(mult, min(n, cap) + 1, mult):
        if n % t == 0:
            best = t
    assert best is not None, (n, cap, mult)
    return best


def _disc_kernel(ar_ref, ai_ref, ldt_ref, br_ref, bi_ref, abr_ref, abi_ref, bbr_ref, bbi_ref):
    ar = ar_ref[...]
    ai = ai_ref[...]
    dt = jnp.exp(ldt_ref[...])
    mag = jnp.exp(ar * dt)
    abar_re = mag * jnp.cos(ai * dt)
    abar_im = mag * jnp.sin(ai * dt)
    den = ar * ar + ai * ai
    nr = abar_re - 1.0
    coef_re = (nr * ar + abar_im * ai) / den
    coef_im = (abar_im * ar - nr * ai) / den
    br = br_ref[...]
    bi = bi_ref[...]
    abr_ref[...] = abar_re
    abi_ref[...] = abar_im
    bbr_ref[...] = coef_re * br - coef_im * bi
    bbi_ref[...] = coef_re * bi + coef_im * br


def _s5_discretize(a_re, a_im, log_dt, b_re, b_im):
    g, p, m = b_re.shape
    rep = lambda a: jnp.repeat(a.astype(F32), m, axis=1)
    flat = lambda b: b.astype(F32).reshape(g, p * m)
    shp = jax.ShapeDtypeStruct((g, p * m), F32)
    abr, abi, bbr, bbi = pl.pallas_call(
        _disc_kernel, out_shape=(shp, shp, shp, shp), name="s5_disc",
    )(rep(a_re), rep(a_im), rep(log_dt), flat(b_re), flat(b_im))
    return abr[:, ::m], abi[:, ::m], bbr.reshape(g, p, m), bbi.reshape(g, p, m)


def _proj_kernel(x_ref, tab_ref, gmix_ref, wlat_ref, gq_ref, wuqn_ref, wuqr_ref, gkv_ref, wu_ref,
                 q_ref, ckv_ref, kr_ref, u_ref, *, q_lora, kv_lora, n_heads, scale):
    x = x_ref[...]
    xn = (_rms(x) * gmix_ref[...]).astype(BF16)
    z = jnp.dot(xn, wlat_ref[...], preferred_element_type=F32)
    cqn = (_rms(z[:, :q_lora]) * gq_ref[...]).astype(BF16)
    cos2, sin2, tab_k = tab_ref[:, :LANES], tab_ref[:, LANES:2 * LANES], tab_ref[:, 2 * LANES:]
    first_half = (lax.broadcasted_iota(jnp.int32, cos2.shape, 1) % 64) < 32
    for j in range(n_heads // 2):
        qn = (jnp.dot(cqn, wuqn_ref[:, j * 256:(j + 1) * 256], preferred_element_type=F32) * scale).astype(BF16)
        q_ref[2 * j, :, 0:LANES] = qn[:, :LANES]
        q_ref[2 * j + 1, :, 0:LANES] = qn[:, LANES:]
    for j in range(n_heads // 4):
        qr = jnp.dot(cqn, wuqr_ref[:, j * 256:(j + 1) * 256], preferred_element_type=F32) * scale
        for e in range(2):
            xr = qr[:, e * LANES:(e + 1) * LANES]
            swapped = jnp.where(first_half, pltpu.roll(xr, 96, 1), pltpu.roll(xr, 32, 1))
            rot = xr * cos2 + swapped * sin2
            ha = 4 * j + 2 * e
            q_ref[ha, :, LANES:2 * LANES] = rot.astype(BF16)
            q_ref[ha + 1, :, LANES:2 * LANES] = pltpu.roll(rot, 64, 1).astype(BF16)
    ckv_ref[...] = _rms(z[:, q_lora:q_lora + kv_lora]) * gkv_ref[...]
    kr = z[:, q_lora + kv_lora:] * tab_k
    kr = kr + pltpu.roll(kr, 64, 1)
    kr_ref[...] = kr[:, :64]
    u_ref[...] = jnp.dot(xn, wu_ref[...], preferred_element_type=F32)


def _proj(x2, tab, tab_blocks, gmix, wlat, gq, wuqn, wuqr, gkv, wu, *, tile, n_heads, q_lora, kv_lora, rope, scale):
    t_rows, d = x2.shape
    n_tiles = t_rows // tile
    kern = functools.partial(_proj_kernel, q_lora=q_lora, kv_lora=kv_lora, n_heads=n_heads, scale=scale)
    return pl.pallas_call(
        kern,
        grid=(n_tiles,),
        in_specs=[
            pl.BlockSpec((tile, d), lambda r: (r, 0)),
            pl.BlockSpec((tile, tab.shape[1]), lambda r: (r % tab_blocks, 0)),
            _const_spec(gmix.shape), _const_spec(wlat.shape), _const_spec(gq.shape),
            _const_spec(wuqn.shape), _const_spec(wuqr.shape), _const_spec(gkv.shape), _const_spec(wu.shape),
        ],
        out_specs=[
            pl.BlockSpec((n_heads, tile, 2 * LANES), lambda r: (0, r, 0)),
            pl.BlockSpec((tile, kv_lora), lambda r: (r, 0)),
            pl.BlockSpec((tile, rope), lambda r: (r, 0)),
            pl.BlockSpec((tile, wu.shape[1]), lambda r: (r, 0)),
        ],
        out_shape=[
            jax.ShapeDtypeStruct((n_heads, t_rows, 2 * LANES), BF16),
            jax.ShapeDtypeStruct((t_rows, kv_lora), F32),
            jax.ShapeDtypeStruct((t_rows, rope), F32),
            jax.ShapeDtypeStruct((t_rows, wu.shape[1]), F32),
        ],
        compiler_params=_cparams(1),
        name="proj",
    )(x2, tab, gmix, wlat, gq, wuqn, wuqr, gkv, wu)


def _kvup_kernel(ckv_ref, kr_ref, wuk_ref, wuv_ref, k_ref, v_ref, *, n_heads):
    c = ckv_ref[...].astype(BF16)
    kr = kr_ref[...]
    krz = jnp.concatenate([kr, jnp.zeros_like(kr)], axis=1).astype(BF16)
    ones = jnp.ones((c.shape[0], LANES), BF16)
    for hp in range(n_heads // 2):
        kn = jnp.dot(c, wuk_ref[:, hp * 256:(hp + 1) * 256], preferred_element_type=F32).astype(BF16)
        vv = jnp.dot(c, wuv_ref[:, hp * 256:(hp + 1) * 256], preferred_element_type=F32).astype(BF16)
        for s in range(2):
            h = 2 * hp + s
            k_ref[h, :, 0:LANES] = kn[:, s * LANES:(s + 1) * LANES]
            k_ref[h, :, LANES:2 * LANES] = krz
            v_ref[h, :, 0:LANES] = vv[:, s * LANES:(s + 1) * LANES]
            v_ref[h, :, LANES:2 * LANES] = ones


def _kvup(ckv2, kr2, wuk, wuv, *, tile, n_heads):
    t_rows, kv_lora = ckv2.shape
    rope = kr2.shape[1]
    kern = functools.partial(_kvup_kernel, n_heads=n_heads)
    return pl.pallas_call(
        kern,
        grid=(t_rows // tile,),
        in_specs=[
            pl.BlockSpec((tile, kv_lora), lambda r: (r, 0)),
            pl.BlockSpec((tile, rope), lambda r: (r, 0)),
            _const_spec(wuk.shape), _const_spec(wuv.shape),
        ],
        out_specs=[
            pl.BlockSpec((n_heads, tile, 2 * LANES), lambda r: (0, r, 0)),
            pl.BlockSpec((n_heads, tile, 2 * LANES), lambda r: (0, r, 0)),
        ],
        out_shape=[
            jax.ShapeDtypeStruct((n_heads, t_rows, 2 * LANES), BF16),
            jax.ShapeDtypeStruct((n_heads, t_rows, 2 * LANES), BF16),
        ],
        compiler_params=_cparams(1),
        name="kvup",
    )(ckv2, kr2, wuk, wuv)


def _attn_tile(q, k, v1, m_ref, acc_ref, mask):
    s = lax.dot_general(q, k, (((1,), (1,)), ((), ())), preferred_element_type=F32)
    if mask is not None:
        s = jnp.where(mask, s, -jnp.inf)
    cols = [s[:, c * LANES:(c + 1) * LANES] for c in range(s.shape[1] // LANES)]
    m_prev = m_ref[...]
    m_new = jnp.maximum(m_prev, jnp.max(functools.reduce(jnp.maximum, cols), axis=-1, keepdims=True))
    alpha = jnp.exp2(m_prev - m_new)
    p = jnp.concatenate([jnp.exp2(c - m_new).astype(BF16) for c in cols], axis=1)
    acc_ref[...] = (jnp.concatenate([alpha, alpha], axis=1) * acc_ref[...]
                    + jnp.dot(p, v1, preferred_element_type=F32))
    m_ref[...] = m_new


def _attn_init(m_ref, acc_ref):
    m_ref[...] = jnp.full(m_ref.shape, -jnp.inf, F32)
    acc_ref[...] = jnp.zeros(acc_ref.shape, F32)


def _attn_finish(acc_ref):
    acc = acc_ref[...]
    return acc[:, :LANES] / acc[:, LANES:]


def _chunk_mask(tq, tk, q_off, k_off):
    qc = (lax.broadcasted_iota(jnp.int32, (tq, tk), 0) + q_off) // CHUNK
    kc = (lax.broadcasted_iota(jnp.int32, (tq, tk), 1) + k_off) // CHUNK
    return kc <= qc


def _attn_causal_kernel(q_ref, k_ref, v_ref, o_ref, *stat_refs, tile, n_chains, n_qblocks):
    ms, accs = stat_refs[0::2], stat_refs[1::2]
    for c in range(n_chains):
        _attn_init(ms[c], accs[c])
    qs = [q_ref[c * tile:(c + 1) * tile, :] for c in range(n_chains)]

    if n_qblocks > 1:
        qi = pl.program_id(2)

        def full_tile(j, carry):
            off = pl.multiple_of(j * tile, tile)
            k = k_ref[pl.ds(off, tile), :]
            v1 = v_ref[pl.ds(off, tile), :]
            for c in range(n_chains):
                _attn_tile(qs[c], k, v1, ms[c], accs[c], None)
            return carry

        lax.fori_loop(0, qi * n_chains, full_tile, 0)
        base = qi * n_chains
    else:
        base = 0
    for d in range(n_chains):
        off = (base + d) * tile
        if n_qblocks > 1:
            off = pl.multiple_of(off, tile)
        k = k_ref[pl.ds(off, tile), :]
        v1 = v_ref[pl.ds(off, tile), :]
        for c in range(d, n_chains):
            _attn_tile(qs[c], k, v1, ms[c], accs[c], _chunk_mask(tile, tile, 0, 0) if c == d else None)
    for c in range(n_chains):
        o_ref[c * tile:(c + 1) * tile, :] = _attn_finish(accs[c]).astype(o_ref.dtype)


def _attention(q, k, v1, *, n_batch, seq, tile, n_chains):
    n_heads = q.shape[0]
    tq = tile * n_chains
    nq = seq // tq
    assert tile % CHUNK == 0 and seq % tq == 0
    stats = []
    for _ in range(n_chains):
        stats += [pltpu.VMEM((tile, LANES), F32), pltpu.VMEM((tile, 2 * LANES), F32)]
    return pl.pallas_call(
        functools.partial(_attn_causal_kernel, tile=tile, n_chains=n_chains, n_qblocks=nq),
        grid=(n_batch, n_heads, nq),
        in_specs=[
            pl.BlockSpec((None, tq, 2 * LANES), lambda b, h, i: (h, b * nq + i, 0)),
            pl.BlockSpec((None, seq, 2 * LANES), lambda b, h, i: (h, b, 0)),
            pl.BlockSpec((None, seq, 2 * LANES), lambda b, h, i: (h, b, 0)),
        ],
        out_specs=pl.BlockSpec((None, tq, LANES), lambda b, h, i: (h, b * nq + i, 0)),
        out_shape=jax.ShapeDtypeStruct((n_heads, n_batch * seq, LANES), BF16),
        scratch_shapes=stats,
        compiler_params=_cparams(3),
        name="attn_causal",
    )(q, k, v1)


def _attn_latent_kernel(q_ref, cc_ref, cn_ref, kc_ref, kn_ref, wuk_ref, wuv_ref, o_ref, *, q_pos0, n_heads):
    ls = q_ref.shape[1]
    nt = (((1,), (1,)), ((), ()))
    ckv = jnp.concatenate([cc_ref[...], cn_ref[...]], axis=0).astype(BF16)
    kr = jnp.concatenate([kc_ref[...], kn_ref[...]], axis=0)
    krz = jnp.concatenate([kr, jnp.zeros_like(kr)], axis=1).astype(BF16)
    q_lat = jnp.concatenate(
        [lax.dot_general(q_ref[h, :, :LANES], wuk_ref[:, h * LANES:(h + 1) * LANES], nt, preferred_element_type=F32)
         for h in range(n_heads)], axis=0).astype(BF16)
    q_rope = jnp.concatenate([q_ref[h, :, LANES:] for h in range(n_heads)], axis=0)
    s = (lax.dot_general(q_lat, ckv, nt, preferred_element_type=F32)
         + lax.dot_general(q_rope, krz, nt, preferred_element_type=F32))
    rows, lk = s.shape
    qc = (lax.rem(lax.broadcasted_iota(jnp.int32, (rows, lk), 0), ls) + q_pos0) // CHUNK
    kc = lax.broadcasted_iota(jnp.int32, (rows, lk), 1) // CHUNK
    s = jnp.where(kc <= qc, s, -jnp.inf)
    p = jnp.exp2(s - jnp.max(s, axis=-1, keepdims=True))
    l = jnp.sum(p, axis=-1, keepdims=True)
    o_lat = (jnp.dot(p.astype(BF16), ckv, preferred_element_type=F32) / l).astype(BF16)
    for h in range(n_heads):
        o_ref[h] = jnp.dot(o_lat[h * ls:(h + 1) * ls, :], wuv_ref[:, h * LANES:(h + 1) * LANES],
                           preferred_element_type=F32).astype(o_ref.dtype)


def _attention_latent(q, ckv_cache, ckv_new, kr_cache, kr_new, wuk, wuv, *, q_pos0):
    n_heads = q.shape[0]
    n_batch, past, kv_lora = ckv_cache.shape
    ls, rope = ckv_new.shape[1], kr_new.shape[2]
    kern = functools.partial(_attn_latent_kernel, q_pos0=q_pos0, n_heads=n_heads)
    return pl.pallas_call(
        kern,
        grid=(n_batch,),
        in_specs=[
            pl.BlockSpec((n_heads, ls, 2 * LANES), lambda b: (0, b, 0)),
            pl.BlockSpec((None, past, kv_lora), lambda b: (b, 0, 0)),
            pl.BlockSpec((None, ls, kv_lora), lambda b: (b, 0, 0)),
            pl.BlockSpec((None, past, rope), lambda b: (b, 0, 0)),
            pl.BlockSpec((None, ls, rope), lambda b: (b, 0, 0)),
            _const_spec(wuk.shape), _const_spec(wuv.shape),
        ],
        out_specs=pl.BlockSpec((n_heads, ls, LANES), lambda b: (0, b, 0)),
        out_shape=jax.ShapeDtypeStruct((n_heads, n_batch * ls, LANES), BF16),
        compiler_params=_cparams(1),
        name="attn_latent",
    )(q, ckv_cache, ckv_new, kr_cache, kr_new, wuk, wuv)


def _s5_kernel(u_ref, h0_ref, a_ref, wb_ref, wc_ref, d_ref, g_ref, hout_ref, *scratch, tl, n_batch, ns):
    i = pl.program_id(1)
    half = a_ref.shape[-1] // 2
    rows = tl * n_batch
    x_refs, hs_ref = scratch[:ns], scratch[ns]

    @pl.when(i == 0)
    def _():
        hs_ref[...] = h0_ref[...]

    ut = jnp.swapaxes(u_ref[...], 0, 1).reshape(rows, ns * LANES)
    for s in range(ns):
        x_ref = x_refs[s]
        u = ut[:, s * LANES:(s + 1) * LANES]
        ub = u.astype(BF16)
        x_ref[:, :half] = jnp.dot(ub, wb_ref[s, :, :half], preferred_element_type=F32)
        x_ref[:, half:] = jnp.dot(ub, wb_ref[s, :, half:], preferred_element_type=F32)
        ar = jnp.broadcast_to(a_ref[s, :, :half], (n_batch, half))
        ai = jnp.broadcast_to(a_ref[s, :, half:], (n_batch, half))
        hr, hi = hs_ref[s, :, :half], hs_ref[s, :, half:]
        for t in range(tl):
            r0 = t * n_batch
            nr = ar * hr - ai * hi + x_ref[r0:r0 + n_batch, :half]
            ni = ar * hi + ai * hr + x_ref[r0:r0 + n_batch, half:]
            x_ref[r0:r0 + n_batch, :half] = nr
            x_ref[r0:r0 + n_batch, half:] = ni
            hr, hi = nr, ni
        hs_ref[s, :, :half] = hr
        hs_ref[s, :, half:] = hi

        y = (jnp.dot(x_ref[:, :half].astype(BF16), wc_ref[s, :half, :], preferred_element_type=F32)
             + jnp.dot(x_ref[:, half:].astype(BF16), wc_ref[s, half:, :], preferred_element_type=F32))
        g = jax.nn.gelu(y + d_ref[s] * u)
        g_ref[:, :, s * LANES:(s + 1) * LANES] = jnp.swapaxes(
            g.reshape(tl, n_batch, LANES), 0, 1).astype(g_ref.dtype)

    @pl.when(i == pl.num_programs(1) - 1)
    def _():
        hout_ref[...] = hs_ref[...]


def _s5(u3, h0, a_l, wb, wc, d_l, *, tl, ns):
    n_batch, seq, width = u3.shape
    n_slabs = width // LANES
    st = a_l.shape[-1]
    kern = functools.partial(_s5_kernel, tl=tl, n_batch=n_batch, ns=ns)
    return pl.pallas_call(
        kern,
        grid=(n_slabs // ns, seq // tl),
        in_specs=[
            pl.BlockSpec((n_batch, tl, ns * LANES), lambda j, i: (0, i, j)),
            pl.BlockSpec((ns, n_batch, st), lambda j, i: (j, 0, 0)),
            pl.BlockSpec((ns, 1, st), lambda j, i: (j, 0, 0)),
            pl.BlockSpec((ns, LANES, st), lambda j, i: (j, 0, 0)),
            pl.BlockSpec((ns, st, LANES), lambda j, i: (j, 0, 0)),
            pl.BlockSpec((ns, 1, LANES), lambda j, i: (j, 0, 0)),
        ],
        out_specs=[
            pl.BlockSpec((n_batch, tl, ns * LANES), lambda j, i: (0, i, j)),
            pl.BlockSpec((ns, n_batch, st), lambda j, i: (j, 0, 0)),
        ],
        out_shape=[
            jax.ShapeDtypeStruct((n_batch, seq, width), BF16),
            jax.ShapeDtypeStruct((n_slabs, n_batch, st), F32),
        ],
        scratch_shapes=[pltpu.VMEM((tl * n_batch, st), F32) for _ in range(ns)] + [pltpu.VMEM((ns, n_batch, st), F32)],
        compiler_params=_cparams(2),
        name="s5",
    )(u3, h0, a_l, wb, wc, d_l)


def _merge_kernel(x_ref, g_ref, attn_ref, gmix_ref, wga_ref, wgb_ref, wa_ref, wb_ref, wo_ref,
                  h_ref, xn_ref, acc_ref):
    j = pl.program_id(1)

    @pl.when(j == 0)
    def _():
        xn_ref[...] = (_rms(x_ref[...]) * gmix_ref[...]).astype(BF16)
        acc_ref[...] = jnp.zeros(acc_ref.shape, F32)

    xn = xn_ref[...]
    g = g_ref[...]
    ga = jax.nn.sigmoid(jnp.dot(xn, wga_ref[...], preferred_element_type=F32))
    gb = jax.nn.sigmoid(jnp.dot(xn, wgb_ref[...], preferred_element_type=F32))
    s5o = (jnp.dot(g, wa_ref[...], preferred_element_type=F32)
           * jax.nn.sigmoid(jnp.dot(g, wb_ref[...], preferred_element_type=F32)))
    attn = jnp.concatenate([attn_ref[h].astype(F32) for h in range(attn_ref.shape[0])], axis=1)
    mix = (ga * attn + gb * s5o).astype(BF16)
    acc_ref[...] += jnp.dot(mix, wo_ref[...], preferred_element_type=F32)

    @pl.when(j == pl.num_programs(1) - 1)
    def _():
        h_ref[...] = x_ref[...] + acc_ref[...]


def _merge(x2, g2, attn, gmix, wga, wgb, wa, wb, wo, *, tile, cb):
    t_rows, d = x2.shape
    hpb = cb // LANES
    return pl.pallas_call(
        _merge_kernel,
        grid=(t_rows // tile, d // cb),
        in_specs=[
            pl.BlockSpec((tile, d), lambda r, j: (r, 0)),
            pl.BlockSpec((tile, d), lambda r, j: (r, 0)),
            pl.BlockSpec((hpb, tile, LANES), lambda r, j: (j, r, 0)),
            _const_spec(gmix.shape),
            pl.BlockSpec((d, cb), lambda r, j: (0, j)),
            pl.BlockSpec((d, cb), lambda r, j: (0, j)),
            pl.BlockSpec((d, cb), lambda r, j: (0, j)),
            pl.BlockSpec((d, cb), lambda r, j: (0, j)),
            pl.BlockSpec((cb, d), lambda r, j: (j, 0)),
        ],
        out_specs=pl.BlockSpec((tile, d), lambda r, j: (r, 0)),
        out_shape=jax.ShapeDtypeStruct((t_rows, d), F32),
        scratch_shapes=[pltpu.VMEM((tile, d), BF16), pltpu.VMEM((tile, d), F32)],
        compiler_params=_cparams(2),
        name="merge",
    )(x2, g2, attn, gmix, wga, wgb, wa, wb, wo)


def _mlp_kernel(h_ref, gmlp_ref, wup_ref, wdn_ref, gfin_ref, y_ref, hn_ref, acc_ref, *, final_norm):
    j = pl.program_id(1)

    @pl.when(j == 0)
    def _():
        hn_ref[...] = (_rms(h_ref[...]) * gmlp_ref[...]).astype(BF16)
        acc_ref[...] = jnp.zeros(acc_ref.shape, F32)

    a = jnp.maximum(jnp.dot(hn_ref[...], wup_ref[...], preferred_element_type=F32), 0.0)
    acc_ref[...] += jnp.dot((a * a).astype(BF16), wdn_ref[...], preferred_element_type=F32)

    @pl.when(j == pl.num_programs(1) - 1)
    def _():
        y = h_ref[...] + acc_ref[...]
        y_ref[...] = _rms(y) * gfin_ref[...] if final_norm else y


def _mlp(h2, gmlp, wup, wdn, gfin, *, tile, fb, final_norm):
    t_rows, d = h2.shape
    d_ff = wup.shape[1]
    return pl.pallas_call(
        functools.partial(_mlp_kernel, final_norm=final_norm),
        grid=(t_rows // tile, d_ff // fb),
        in_specs=[
            pl.BlockSpec((tile, d), lambda r, j: (r, 0)),
            _const_spec(gmlp.shape),
            pl.BlockSpec((d, fb), lambda r, j: (0, j)),
            pl.BlockSpec((fb, d), lambda r, j: (j, 0)),
            _const_spec(gfin.shape),
        ],
        out_specs=pl.BlockSpec((tile, d), lambda r, j: (r, 0)),
        out_shape=jax.ShapeDtypeStruct((t_rows, d), F32),
        scratch_shapes=[pltpu.VMEM((tile, d), BF16), pltpu.VMEM((tile, d), F32)],
        compiler_params=_cparams(2),
        name="mlp",
    )(h2, gmlp, wup, wdn, gfin)


def _rope_table(pos, rope):
    half = rope // 2
    inv = ROPE_THETA ** (-jnp.arange(half, dtype=F32) / half)
    ang = pos.astype(F32)[:, None] * inv[None, :]
    cos, sin = jnp.cos(ang), jnp.sin(ang)
    return jnp.concatenate([cos, cos, cos, cos, -sin, sin, -sin, sin, cos, cos, -sin, sin], axis=1)


def _swap_halves(w):
    half = w.shape[-1] // 2
    return jnp.concatenate([w[..., half:], w[..., :half]], axis=-1)


def _tiles(n_batch, seq, causal):
    rows = n_batch * seq
    t = dict(
        proj=_pick_tile(seq, PROJ_ROWS, 16) if seq >= PROJ_ROWS else rows,
        merge=_pick_tile(rows, MERGE_ROWS, 16),
        mlp=_pick_tile(rows, MLP_ROWS, 16),
        s5_tl=_pick_tile(seq, S5_TL, SUBLANES),
        attn=None, chains=1,
    )
    if causal:
        t["attn"] = _pick_tile(seq, ATTN_TILE, CHUNK)
        t["chains"] = max(c for c in range(1, ATTN_CHAINS + 1) if (seq // t["attn"]) % c == 0)
    return t


def _group(x, pos0, w, s5w, h0, cache):
    n_batch, seq, d = x.shape
    t_rows = n_batch * seq
    n_heads, kv_lora, rope = w["n_heads"], w["kv_lora"], w["rope"]
    x2 = x.reshape(t_rows, d)
    tiles = _tiles(n_batch, seq, causal=cache is None)
    row_tile = tiles["proj"]

    pos = pos0 + jnp.arange(seq)
    tab = _rope_table(pos, rope)
    if row_tile > seq:
        assert row_tile % seq == 0
        tab = jnp.tile(tab, (row_tile // seq, 1))
        tab_blocks = 1
    else:
        assert seq % row_tile == 0
        tab_blocks = seq // row_tile

    q, ckv, kr, u = _proj(x2, tab, tab_blocks, w["gmix"], w["wlat"], w["gq"], w["wuqn"], w["wuqr"], w["gkv"], w["wu"],
                          tile=row_tile, n_heads=n_heads, q_lora=w["q_lora"], kv_lora=kv_lora, rope=rope,
                          scale=w["scale"])
    ckv3 = ckv.reshape(n_batch, seq, kv_lora)
    kr3 = kr.reshape(n_batch, seq, rope)
    if cache is not None:
        attn = _attention_latent(q, cache[0].astype(F32), ckv3, cache[1].astype(F32), kr3, w["wuk"], w["wuv"],
                                 q_pos0=pos0)
    else:
        k, v1 = _kvup(ckv, kr, w["wuk"], w["wuv"], tile=_pick_tile(t_rows, KV_ROWS, 16), n_heads=n_heads)
        attn = _attention(q, k, v1, n_batch=n_batch, seq=seq, tile=tiles["attn"], n_chains=tiles["chains"])

    n_slabs = s5w["a"].shape[0]
    g, h_fin = _s5(u.reshape(n_batch, seq, -1), h0, s5w["a"], s5w["wb"], s5w["wc"], s5w["d"], tl=tiles["s5_tl"],
                   ns=math.gcd(n_slabs, S5_SLABS))

    h = _merge(x2, g.reshape(t_rows, -1), attn, w["gmix"], w["wga"], w["wgb"], w["wglu_a"], w["wglu_b"], w["wo"],
               tile=tiles["merge"], cb=4 * LANES)
    return h, ckv3, kr3, h_fin, tiles["mlp"]


def _state_to_slabs(s_re, s_im, n_slabs):
    n_batch, g, p = s_re.shape
    f = lambda s: s.astype(F32).reshape(n_batch, n_slabs, (g // n_slabs) * p).swapaxes(0, 1)
    return jnp.concatenate([f(s_re), f(s_im)], axis=-1)


def _slabs_to_state(h, g, p):
    n_slabs, n_batch, st = h.shape
    half = st // 2
    f = lambda s: s.swapaxes(0, 1).reshape(n_batch, g, p)
    return f(h[..., :half]), f(h[..., half:])


def kernel(x_prompt, x_sample, cache_ckv, cache_krope, state_s5_re, state_s5_im, norm_mix, w_in, norm_q, w_uq, norm_kv, w_uk, w_uv, s5_a_re, s5_a_im, s5_log_dt, s5_b_re, s5_b_im, s5_c_re, s5_c_im, s5_d, w_glu_a, w_glu_b, w_o, norm_mlp, w_up, w_down, norm_final):
    depth = w_in.shape[0]
    d = x_prompt.shape[-1]
    q_lora, kv_lora = norm_q.shape[1], norm_kv.shape[1]
    n_heads, nope = w_uk.shape[2], w_uk.shape[3]
    v_dim = w_uv.shape[3]
    rope = cache_krope.shape[-1]
    n_groups, n_state, grp = s5_b_re.shape[1:]
    width = n_groups * grp
    past_len = cache_ckv.shape[2]
    sdt = state_s5_re.dtype
    assert nope == LANES and v_dim == LANES and 2 * rope == LANES and d % (4 * LANES) == 0
    assert width % LANES == 0 and LANES % grp == 0 and x_prompt.shape[0] == SUBLANES == x_sample.shape[0]
    n_slabs = width // LANES
    gps = n_groups // n_slabs
    half = gps * n_state
    scale = math.log2(math.e) / math.sqrt(nope + rope)
    c1 = q_lora + kv_lora
    c2 = c1 + rope
    c3 = c2 + width
    c4 = c3 + d

    hp, hs = x_prompt, x_sample
    outs_p, outs_s = [], []
    for l in range(depth):
        wi = w_in[l]
        w_kr = wi[:, c1:c2]
        wuq3 = w_uq[l].reshape(q_lora, n_heads, nope + rope)
        w = dict(
            n_heads=n_heads, q_lora=q_lora, kv_lora=kv_lora, rope=rope, scale=scale,
            gmix=norm_mix[l][None].astype(F32), gq=norm_q[l][None].astype(F32), gkv=norm_kv[l][None].astype(F32),
            wlat=jnp.concatenate([wi[:, :c1], w_kr, _swap_halves(w_kr)], axis=1).astype(BF16),
            wuqn=wuq3[..., :nope].reshape(q_lora, n_heads * nope).astype(BF16),
            wuqr=wuq3[..., nope:].reshape(q_lora, n_heads * rope).astype(BF16),
            wu=wi[:, c2:c3].astype(BF16), wga=wi[:, c3:c4].astype(BF16), wgb=wi[:, c4:].astype(BF16),
            wuk=w_uk[l].reshape(kv_lora, n_heads * nope).astype(BF16),
            wuv=w_uv[l].reshape(kv_lora, n_heads * v_dim).astype(BF16),
            wglu_a=w_glu_a[l].astype(BF16), wglu_b=w_glu_b[l].astype(BF16), wo=w_o[l].astype(BF16),
        )
        abr, abi, bbr, bbi = _s5_discretize(s5_a_re[l], s5_a_im[l], s5_log_dt[l], s5_b_re[l], s5_b_im[l])
        on_diag = jnp.eye(gps, dtype=jnp.bool_)[None, :, None, :, None]
        bd_in = lambda b: jnp.where(
            on_diag, b.reshape(n_slabs, gps, n_state, grp).swapaxes(2, 3)[:, :, :, None, :], 0.0).reshape(
                n_slabs, LANES, half)
        bd_out = lambda c: jnp.where(
            on_diag, c.astype(F32).reshape(n_slabs, gps, grp, n_state).swapaxes(2, 3)[:, :, :, None, :], 0.0).reshape(
                n_slabs, half, LANES)
        s5w = dict(
            a=jnp.concatenate([abr.reshape(n_slabs, 1, half), abi.reshape(n_slabs, 1, half)], axis=-1),
            wb=jnp.concatenate([bd_in(bbr), bd_in(bbi)], axis=-1).astype(BF16),
            wc=jnp.concatenate([bd_out(s5_c_re[l]), -bd_out(s5_c_im[l])], axis=1).astype(BF16),
            d=s5_d[l].astype(F32).reshape(n_slabs, 1, LANES),
        )
        gmlp, gfin = norm_mlp[l][None].astype(F32), norm_final[None].astype(F32)
        wup, wdn = w_up[l].astype(BF16), w_down[l].astype(BF16)

        bp, lp, _ = hp.shape
        h0 = jnp.zeros((n_slabs, bp, 2 * half), F32)
        last = l == depth - 1
        h, ckv, kr, hfin, mlp_tile = _group(hp, 0, w, s5w, h0, None)
        outs_p.append((ckv, kr) + _slabs_to_state(hfin, n_groups, n_state))
        hp = _mlp(h, gmlp, wup, wdn, gfin, tile=mlp_tile, fb=FF_BLOCK, final_norm=last).reshape(bp, lp, d)

        bs, ls, _ = hs.shape
        h0 = _state_to_slabs(state_s5_re[l], state_s5_im[l], n_slabs)
        h, ckv, kr, hfin, mlp_tile = _group(hs, past_len, w, s5w, h0, (cache_ckv[l], cache_krope[l]))
        outs_s.append((ckv, kr) + _slabs_to_state(hfin, n_groups, n_state))
        hs = _mlp(h, gmlp, wup, wdn, gfin, tile=mlp_tile, fb=FF_BLOCK, final_norm=last).reshape(bs, ls, d)

    def stack(outs, i, dt):
        if len(outs) == 1:
            return outs[0][i].astype(dt)[None]
        return jnp.stack([o[i].astype(dt) for o in outs], axis=0)
    return (hp, hs,
            stack(outs_p, 0, F32), stack(outs_p, 1, F32), stack(outs_p, 2, sdt), stack(outs_p, 3, sdt),
            stack(outs_s, 0, F32), stack(outs_s, 1, F32), stack(outs_s, 2, sdt), stack(outs_s, 3, sdt))
```
